```python
import math
import jax, jax.numpy as jnp
from jax import lax
import numpy as np

D_MODEL = 1024
BATCH = 4
SEQ = 4096
DEPTH = 1

CHUNK = 64
CONV_W = D_MODEL // 2
CONV_K = 3
N_HEADS = 8
HEAD_DIM = 64
ATT_W = N_HEADS * HEAD_DIM
IDX_HEADS = 4
IDX_DIM = 64
IDX_TOPK = 256
IDX_SCALE = (IDX_HEADS * IDX_DIM) ** -0.5
SPLIT_SIZES = (CONV_W, CONV_W, CONV_W,
               ATT_W, ATT_W, ATT_W,
               IDX_HEADS * IDX_DIM, IDX_DIM, IDX_HEADS,
               D_MODEL, D_MODEL)
IN_W = sum(SPLIT_SIZES)
N_EXPERTS = 32
TOP_K = 4
D_FF = D_MODEL
SWIGLU_LIMIT = 7.0
SWIGLU_ALPHA = 1.702
EXPERT_BLOCK = 256
EPS = 1e-6

kernel_name = 'hybrid_conv_dsa_moe_adaln_block'


def rms_norm(x, g):
    xf = x.astype(jnp.float32)
    y = xf * lax.rsqrt(jnp.mean(xf * xf, axis=-1, keepdims=True) + EPS)
    return (y * g.astype(jnp.float32)).astype(x.dtype)


def causal_depthwise_conv(u, w):
    return lax.conv_general_dilated(
        u, w[:, None, :].astype(u.dtype), window_strides=(1,),
        padding=[(CONV_K - 1, 0)], dimension_numbers=('NWC', 'WIO', 'NWC'),
        feature_group_count=u.shape[-1])


def dsa_sparse_attention(q, k, v, qi, ki, wi):
    bsz, seq = q.shape[0], q.shape[1]
    topk = min(IDX_TOPK, seq // 4)
    nblk = seq // CHUNK
    key_pos = jnp.arange(seq)
    kif = ki.astype(jnp.float32)
    gather = jax.vmap(lambda a, i: a[i])

    def to_blocks(a):
        return jnp.moveaxis(a.reshape(bsz, nblk, CHUNK, *a.shape[2:]), 1, 0)

    def block(args):
        qb, qib, wib, j = args
        limit = (j + 1) * CHUNK
        dots = jnp.einsum('bqhd,bsd->bqhs', qib.astype(jnp.float32), kif)
        score = jnp.einsum('bqh,bqhs->bqs', wib.astype(jnp.float32) * IDX_SCALE, jax.nn.relu(dots))
        score = jnp.where(key_pos[None, None, :] < limit, score, -jnp.inf)
        _, sel = lax.top_k(score, topk)
        valid = sel < limit
        k_sel = gather(k, sel)
        v_sel = gather(v, sel)
        logits = jnp.einsum('bqhd,bqkhd->bqhk', qb, k_sel).astype(jnp.float32) * HEAD_DIM ** -0.5
        logits = jnp.where(valid[:, :, None, :], logits, -jnp.inf)
        p = jax.nn.softmax(logits, axis=-1).astype(v.dtype)
        return jnp.einsum('bqhk,bqkhd->bqhd', p, v_sel)

    out = lax.map(block, (to_blocks(q), to_blocks(qi), to_blocks(wi), jnp.arange(nblk)))
    return jnp.moveaxis(out, 0, 1).reshape(bsz, seq, ATT_W)


def hybrid_mixer(h, w_in, conv_w, q_norm_g, k_norm_g, w_conv_out, w_attn_out, w_o):
    bsz, seq, _ = h.shape
    z = jnp.einsum('bsd,de->bse', h, w_in)
    cuts = list(np.cumsum(SPLIT_SIZES)[:-1])
    xin, gb, gc, q, k, v, qi, ki, wi, g_a, g_b = jnp.split(z, cuts, axis=-1)
    y_a = gb * causal_depthwise_conv(gc * xin, conv_w)
    q = rms_norm(q.reshape(bsz, seq, N_HEADS, HEAD_DIM), q_norm_g)
    k = rms_norm(k.reshape(bsz, seq, N_HEADS, HEAD_DIM), k_norm_g)
    v = v.reshape(bsz, seq, N_HEADS, HEAD_DIM)
    qi = qi.reshape(bsz, seq, IDX_HEADS, IDX_DIM)
    y_b = dsa_sparse_attention(q, k, v, qi, ki, wi)
    merged = (jax.nn.sigmoid(g_a) * jnp.einsum('bsc,cd->bsd', y_a, w_conv_out)
              + jax.nn.sigmoid(g_b) * jnp.einsum('bsc,cd->bsd', y_b, w_attn_out))
    return jnp.einsum('bsd,de->bse', merged, w_o)


def routed_moe(h, w_router, b_router, w_gate_up, b_gate_up, w_down, b_down):
    bsz, seq, d = h.shape
    xt = h.reshape(-1, d)
    n_tok = xt.shape[0]
    logits = (xt @ w_router + b_router).astype(jnp.float32)
    top_val, top_idx = lax.top_k(logits, TOP_K)
    gates = jax.nn.softmax(top_val, axis=-1)
    nk = n_tok * TOP_K
    flat_e = top_idx.reshape(-1)
    flat_tok = jnp.arange(nk) // TOP_K
    flat_g = gates.reshape(-1)
    order = jnp.argsort(flat_e)
    sorted_e = flat_e[order]
    counts = jnp.bincount(flat_e, length=N_EXPERTS)
    padded = (counts + EXPERT_BLOCK - 1) // EXPERT_BLOCK * EXPERT_BLOCK
    pad_end = jnp.cumsum(padded)
    pad_start = pad_end - padded
    start = jnp.cumsum(counts) - counts
    dest = pad_start[sorted_e] + jnp.arange(nk) - start[sorted_e]
    cap = (-(-nk // EXPERT_BLOCK)) * EXPERT_BLOCK + N_EXPERTS * EXPERT_BLOCK
    n_blocks = cap // EXPERT_BLOCK
    buf_tok = jnp.zeros((cap,), jnp.int32).at[dest].set(flat_tok[order].astype(jnp.int32))
    buf_g = jnp.zeros((cap,), jnp.float32).at[dest].set(flat_g[order])
    block_e = jnp.minimum(
        jnp.searchsorted(pad_end, jnp.arange(n_blocks) * EXPERT_BLOCK, side='right'),
        N_EXPERTS - 1)

    def expert_block(args):
        e, tok, g = args
        xb = xt[tok]
        gu = xb @ w_gate_up[e] + b_gate_up[e]
        x_glu = jnp.minimum(gu[:, :D_FF], SWIGLU_LIMIT)
        x_lin = jnp.clip(gu[:, D_FF:], -SWIGLU_LIMIT, SWIGLU_LIMIT)
        act = x_glu * jax.nn.sigmoid(SWIGLU_ALPHA * x_glu) * (x_lin + 1)
        y = act @ w_down[e] + b_down[e]
        return y * g[:, None].astype(y.dtype)

    ys = lax.map(expert_block, (block_e, buf_tok.reshape(n_blocks, EXPERT_BLOCK),
                                buf_g.reshape(n_blocks, EXPERT_BLOCK)))
    out = jnp.zeros_like(xt).at[buf_tok].add(ys.reshape(cap, d).astype(xt.dtype))
    return out.reshape(bsz, seq, d)


def setup_inputs(seed: int = 0) -> dict:
    key = jax.random.key(seed)
    ks = jax.random.split(key, 20)
    f32 = jnp.float32

    def dense(k, shape, fan_in):
        return jax.random.normal(k, shape, f32) * fan_in ** -0.5

    def gain(k, shape):
        return 1.0 + 0.1 * jax.random.normal(k, shape, f32)

    def bias(k, shape, s):
        return s * jax.random.normal(k, shape, f32)

    L = DEPTH
    return {
        'x': jax.random.normal(ks[0], (BATCH, SEQ, D_MODEL), f32),
        'c': jax.random.normal(ks[1], (BATCH, D_MODEL), f32),
        'w_ada': dense(ks[2], (L, D_MODEL, 6 * D_MODEL), D_MODEL),
        'b_ada': bias(ks[3], (L, 6 * D_MODEL), 0.02),
        'norm1_g': gain(ks[4], (L, D_MODEL)),
        'w_in': dense(ks[5], (L, D_MODEL, IN_W), D_MODEL),
        'conv_w': dense(ks[6], (L, CONV_K, CONV_W), CONV_K),
        'q_norm_g': gain(ks[7], (L, HEAD_DIM)),
        'k_norm_g': gain(ks[8], (L, HEAD_DIM)),
        'w_conv_out': dense(ks[9], (L, CONV_W, D_MODEL), CONV_W),
        'w_attn_out': dense(ks[10], (L, ATT_W, D_MODEL), ATT_W),
        'w_o': dense(ks[11], (L, D_MODEL, D_MODEL), D_MODEL),
        'norm2_g': gain(ks[12], (L, D_MODEL)),
        'w_router': dense(ks[13], (L, D_MODEL, N_EXPERTS), D_MODEL),
        'b_router': bias(ks[14], (L, N_EXPERTS), 0.01),
        'w_gate_up': dense(ks[15], (L, N_EXPERTS, D_MODEL, 2 * D_FF), D_MODEL),
        'b_gate_up': bias(ks[16], (L, N_EXPERTS, 2 * D_FF), 0.02),
        'w_down': dense(ks[17], (L, N_EXPERTS, D_FF, D_MODEL), D_FF),
        'b_down': bias(ks[18], (L, N_EXPERTS, D_MODEL), 0.02),
    }


def reference(x, c, w_ada, b_ada, norm1_g, w_in, conv_w, q_norm_g, k_norm_g,
              w_conv_out, w_attn_out, w_o, norm2_g, w_router, b_router,
              w_gate_up, b_gate_up, w_down, b_down):
    for l in range(DEPTH):
        mod = jnp.einsum('bd,de->be', jax.nn.silu(c), w_ada[l]) + b_ada[l]
        sh1, sc1, g1, sh2, sc2, g2 = [m[:, None, :] for m in jnp.split(mod, 6, axis=-1)]
        h = rms_norm(x, norm1_g[l]) * (1 + sc1) + sh1
        x = x + g1 * hybrid_mixer(h, w_in[l], conv_w[l], q_norm_g[l], k_norm_g[l],
                                  w_conv_out[l], w_attn_out[l], w_o[l])
        h = rms_norm(x, norm2_g[l]) * (1 + sc2) + sh2
        x = x + g2 * routed_moe(h, w_router[l], b_router[l], w_gate_up[l], b_gate_up[l],
                                w_down[l], b_down[l])
    return x
```

```python
import functools

import jax
import jax.numpy as jnp
from jax import lax
from jax.experimental import pallas as pl
from jax.experimental.pallas import tpu as pltpu

F32 = jnp.float32
BF16 = jnp.bfloat16
I32 = jnp.int32

LANES = 128
SUBLANES = 8

CHUNK = 64
CONV_K = 3
N_HEADS = 8
HEAD_DIM = 64
IDX_HEADS = 4
IDX_DIM = 64
IDX_TOPK = 256
N_EXPERTS = 32
TOP_K = 4
SWIGLU_LIMIT = 7.0
SWIGLU_ALPHA = 1.702
EXPERT_BLOCK = 256
EPS = 1e-6

INT_MIN = -(2 ** 31)
VMEM_LIMIT = 56 * 1024 * 1024


def _dot(a, b):
    return jnp.dot(a, b, preferred_element_type=F32)


def _dot_nt(a, b):
    return lax.dot_general(a, b, (((1,), (1,)), ((), ())), preferred_element_type=F32)


def _params(sem, vmem=VMEM_LIMIT):
    return pltpu.CompilerParams(dimension_semantics=sem, vmem_limit_bytes=vmem)


def _ada_kernel(c_ref, w_ref, b_ref, o_ref):
    c = c_ref[...]
    s = c * jax.nn.sigmoid(c)
    o_ref[...] = _dot(s.astype(BF16), w_ref[...].astype(BF16)) + b_ref[...]


def _ada(c_pad, w_ada, b_ada):
    rows, d = c_pad.shape
    n = w_ada.shape[1]
    tn = n // 6
    return pl.pallas_call(
        _ada_kernel,
        out_shape=jax.ShapeDtypeStruct((rows, n), F32),
        grid=(n // tn,),
        in_specs=[pl.BlockSpec((rows, d), lambda j: (0, 0)),
                  pl.BlockSpec((d, tn), lambda j: (0, j)),
                  pl.BlockSpec((1, tn), lambda j: (0, j))],
        out_specs=pl.BlockSpec((rows, tn), lambda j: (0, j)),
        compiler_params=_params(("arbitrary",)),
        name="ada",
    )(c_pad, w_ada, b_ada.reshape(1, n))


def _rms_mod(x, g, sc, sh):
    ms = jnp.mean(x * x, axis=-1, keepdims=True)
    y = x * lax.rsqrt(ms + EPS) * g
    return y * (1.0 + sc) + sh


def _head_rms(t, bd, g):
    sq = t * t
    hi = sq.astype(BF16)
    lo = (sq - hi.astype(F32)).astype(BF16)
    ms = _dot(hi, bd) + _dot(lo, bd)
    return t * lax.rsqrt(ms + EPS) * g


def _inproj_kernel(x_ref, sh_ref, sc_ref, g_ref, w_ref, cw_ref, qg_ref, kg_ref, bd_ref,
                   ya_ref, q_ref, k_ref, v_ref, qi_ref, ki_ref, wi_ref, ubuf, *, tm, cw, aw):
    s = pl.program_id(1)
    hb = _rms_mod(x_ref[0], g_ref[...], sc_ref[0], sh_ref[0]).astype(BF16)

    zc = _dot(hb, w_ref[:, 0:3 * cw])
    xin, gb, gc = zc[:, 0:cw], zc[:, cw:2 * cw], zc[:, 2 * cw:3 * cw]
    u = gc * xin

    @pl.when(s == 0)
    def _():
        ubuf[0:SUBLANES, :] = jnp.zeros((SUBLANES, cw), F32)

    ubuf[SUBLANES:SUBLANES + tm, :] = u
    u1 = ubuf[SUBLANES - 1:SUBLANES - 1 + tm, :]
    u2 = ubuf[SUBLANES - 2:SUBLANES - 2 + tm, :]
    conv = cw_ref[0:1, :] * u2 + cw_ref[1:2, :] * u1 + cw_ref[2:3, :] * u
    ya_ref[0] = (gb * conv).astype(BF16)
    ubuf[0:SUBLANES, :] = ubuf[tm:tm + SUBLANES, :]

    o = 3 * cw
    bd = bd_ref[...]
    q = _dot(hb, w_ref[:, o:o + aw])
    q_ref[0] = (_head_rms(q, bd, qg_ref[...]) * (HEAD_DIM ** -0.5)).astype(BF16)
    k = _dot(hb, w_ref[:, o + aw:o + 2 * aw])
    k_ref[0] = _head_rms(k, bd, kg_ref[...]).astype(BF16)
    v_ref[0] = _dot(hb, w_ref[:, o + 2 * aw:o + 3 * aw]).astype(BF16)

    o = o + 3 * aw
    qw = IDX_HEADS * LANES
    qi_ref[0] = _dot(hb, w_ref[:, o:o + qw]).astype(BF16)
    ki_ref[0] = _dot(hb, w_ref[:, o + qw:o + qw + LANES]).astype(BF16)
    wi_ref[0] = _dot(hb, w_ref[:, o + qw + LANES:o + qw + 2 * LANES]) * ((IDX_HEADS * IDX_DIM) ** -0.5)


def _inproj(x, sh1, sc1, g1n, w_main, conv_w, qg, kg, bd, *, tm):
    b, s, d = x.shape
    cw = conv_w.shape[1]
    aw = N_HEADS * HEAD_DIM
    nw = w_main.shape[1]
    qw = IDX_HEADS * LANES
    row = lambda width: pl.BlockSpec((1, tm, width), lambda bi, si: (bi, si, 0))
    mod = pl.BlockSpec((1, 1, d), lambda bi, si: (bi, 0, 0))
    full = lambda a: pl.BlockSpec(a.shape, lambda bi, si: (0,) * a.ndim)
    outs = [jax.ShapeDtypeStruct((b, s, cw), BF16)] + [jax.ShapeDtypeStruct((b, s, aw), BF16)] * 3 + [
        jax.ShapeDtypeStruct((b, s, qw), BF16), jax.ShapeDtypeStruct((b, s, LANES), BF16),
        jax.ShapeDtypeStruct((b, s, LANES), F32)]
    return pl.pallas_call(
        functools.partial(_inproj_kernel, tm=tm, cw=cw, aw=aw),
        out_shape=outs,
        grid=(b, s // tm),
        in_specs=[row(d), mod, mod, full(g1n), full(w_main), full(conv_w), full(qg), full(kg), full(bd)],
        out_specs=[row(cw), row(aw), row(aw), row(aw), row(qw), row(LANES), row(LANES)],
        scratch_shapes=[pltpu.VMEM((tm + SUBLANES, cw), F32)],
        compiler_params=_params(("arbitrary", "arbitrary")),
        name="inproj",
    )(x, sh1, sc1, g1n, w_main, conv_w, qg, kg, bd)


def _sort_key(score):
    bits = pltpu.bitcast(score, I32)
    return jnp.where(bits < 0, INT_MIN - bits, bits)


def _attn_kernel(q_ref, k_ref, v_ref, qi_ref, ki_ref, wi_ref, ut_ref, o_ref,
                 key_scr, t_scr, need_scr, m_scr, l_scr, acc_scr, *, qb, topk, rb):
    i = pl.program_id(1)
    kt = qb
    ncol = kt // LANES
    npair = N_HEADS // 2
    neg_inf = -jnp.inf

    qi = qi_ref[0]
    q4 = jnp.concatenate([qi[:, h * LANES:(h + 1) * LANES] for h in range(IDX_HEADS)], axis=0)
    wi = wi_ref[0]
    wrep = [jnp.broadcast_to(wi[:, h:h + 1], (qb, kt)) for h in range(IDX_HEADS)]
    row_chunk_end = ((lax.broadcasted_iota(I32, (qb, kt), 0) + i * qb) // CHUNK + 1) * CHUNK

    def score_tile(j, carry):
        off = pl.multiple_of(j * kt, kt)
        dots = _dot_nt(q4, ki_ref[0, pl.ds(off, kt), :])
        sc = wrep[0] * jnp.maximum(dots[0:qb], 0.0)
        for h in range(1, IDX_HEADS):
            sc = sc + wrep[h] * jnp.maximum(dots[h * qb:(h + 1) * qb], 0.0)
        key = _sort_key(sc)
        kpos = lax.broadcasted_iota(I32, (qb, kt), 1) + j * kt
        key = jnp.where(kpos < row_chunk_end, key, INT_MIN)
        key_scr[:, pl.ds(off, kt)] = key
        return carry

    lax.fori_loop(0, i + 1, score_tile, 0)

    def count_ge(r0, cand):
        def body(j, acc):
            off = pl.multiple_of(j * kt, kt)
            for c in range(ncol):
                kk = key_scr[r0:r0 + rb, pl.ds(off + c * LANES, LANES)]
                acc = acc + jnp.where(kk >= cand, 1.0, 0.0)
            return acc
        acc = lax.fori_loop(0, i + 1, body, jnp.zeros((rb, LANES), F32))
        return jnp.broadcast_to(jnp.sum(acc, axis=1, keepdims=True), (rb, LANES))

    @pl.when(i == 0)
    def _():
        t_scr[...] = jnp.full((qb, LANES), INT_MIN + 1, I32)
        need_scr[...] = jnp.full((qb, LANES), float(topk), F32)

    @pl.when(i > 0)
    def _():
        kf = float(topk)
        for r in range(qb // rb):
            r0 = r * rb
            zero = jnp.zeros((rb, LANES), I32)
            res = jnp.where(count_ge(r0, zero) >= kf, zero, jnp.full((rb, LANES), INT_MIN, I32))

            def bit_step(it, res):
                cand = res + lax.shift_left(jnp.int32(1), jnp.int32(30) - it)
                return jnp.where(count_ge(r0, cand) >= kf, cand, res)

            res = lax.fori_loop(0, 31, bit_step, res)
            t_scr[r0:r0 + rb, :] = res
            need_scr[r0:r0 + rb, :] = kf - count_ge(r0, res + 1)

    m_scr[...] = jnp.full(m_scr.shape, neg_inf, F32)
    l_scr[...] = jnp.zeros(l_scr.shape, F32)
    acc_scr[...] = jnp.zeros(acc_scr.shape, F32)
    lane = lax.broadcasted_iota(I32, (qb, LANES), 1)
    low_half = lane < HEAD_DIM
    q = q_ref[0]
    zero_q = jnp.zeros((qb, LANES), BF16)
    qm = []
    for h in range(N_HEADS):
        qp = q[:, (h // 2) * LANES:(h // 2 + 1) * LANES]
        qm.append(jnp.where(low_half if h % 2 == 0 else jnp.logical_not(low_half), qp, zero_q))
    thr = jnp.concatenate([t_scr[...]] * ncol, axis=1)
    need = jnp.concatenate([need_scr[...]] * ncol, axis=1)
    ut = ut_ref[...]

    def attn_tile(j, run):
        off = pl.multiple_of(j * kt, kt)
        key = key_scr[:, pl.ds(off, kt)]
        eq = key == thr
        cnt = _dot(jnp.where(eq, 1.0, 0.0).astype(BF16), ut) + run
        sel = jnp.logical_or(key > thr, jnp.logical_and(eq, cnt <= need))
        bias = jnp.where(sel, 0.0, neg_inf)
        for h in range(N_HEADS):
            p2 = h // 2
            kp = k_ref[0, pl.ds(off, kt), p2 * LANES:(p2 + 1) * LANES]
            vp = v_ref[0, pl.ds(off, kt), p2 * LANES:(p2 + 1) * LANES]
            s = _dot_nt(qm[h], kp) + bias
            m_old = m_scr[h]
            m_new = jnp.maximum(m_old, jnp.broadcast_to(jnp.max(s, axis=1, keepdims=True), (qb, LANES)))
            m_safe = jnp.where(m_new == neg_inf, 0.0, m_new)
            alpha = jnp.exp(m_old - m_safe)
            p = jnp.exp(s - jnp.concatenate([m_safe] * ncol, axis=1))
            l_scr[h] = alpha * l_scr[h] + jnp.broadcast_to(jnp.sum(p, axis=1, keepdims=True), (qb, LANES))
            acc_scr[h] = alpha * acc_scr[h] + _dot(p.astype(BF16), vp)
            m_scr[h] = m_new
        return jnp.broadcast_to(cnt[:, kt - 1:kt], (qb, kt))

    lax.fori_loop(0, i + 1, attn_tile, jnp.zeros((qb, kt), F32))

    for p2 in range(npair):
        lo = acc_scr[2 * p2] / l_scr[2 * p2]
        hi = acc_scr[2 * p2 + 1] / l_scr[2 * p2 + 1]
        o_ref[0, :, p2 * LANES:(p2 + 1) * LANES] = jnp.where(low_half, lo, hi).astype(BF16)


def _attn(q, k, v, qi, ki, wi, ut, *, qb):
    b, s, aw = q.shape
    topk = min(IDX_TOPK, s // 4)
    assert topk == qb and s % qb == 0 and qb % CHUNK == 0
    rowq = lambda width: pl.BlockSpec((1, qb, width), lambda bi, i: (bi, i, 0))
    allk = lambda width: pl.BlockSpec((1, s, width), lambda bi, i: (bi, 0, 0))
    return pl.pallas_call(
        functools.partial(_attn_kernel, qb=qb, topk=topk, rb=64),
        out_shape=jax.ShapeDtypeStruct((b, s, aw), BF16),
        grid=(b, s // qb),
        in_specs=[rowq(aw), allk(aw), allk(aw), rowq(qi.shape[2]), allk(LANES), rowq(LANES),
                  pl.BlockSpec(ut.shape, lambda bi, i: (0, 0))],
        out_specs=rowq(aw),
        scratch_shapes=[pltpu.VMEM((qb, s), I32), pltpu.VMEM((qb, LANES), I32), pltpu.VMEM((qb, LANES), F32),
                        pltpu.VMEM((N_HEADS, qb, LANES), F32), pltpu.VMEM((N_HEADS, qb, LANES), F32),
                        pltpu.VMEM((N_HEADS, qb, LANES), F32)],
        compiler_params=_params(("arbitrary", "arbitrary")),
        name="attn",
    )(q, k, v, qi, ki, wi, ut)


def _merge_kernel(x_ref, ya_ref, yb_ref, sh1_ref, sc1_ref, g1_ref, sh2_ref, sc2_ref, n1_ref, n2_ref,
                  wg_ref, pa_ref, pb_ref, wo_ref, wr_ref, br_ref, tri_ref,
                  x1_ref, h2_ref, meta_ref, cnt_ref, carry, *, tm, d):
    first = jnp.logical_and(pl.program_id(0) == 0, pl.program_id(1) == 0)

    @pl.when(first)
    def _():
        carry[...] = jnp.zeros(carry.shape, F32)

    x = x_ref[0]
    hb = _rms_mod(x, n1_ref[...], sc1_ref[0], sh1_ref[0]).astype(BF16)
    ga = jax.nn.sigmoid(_dot(hb, wg_ref[:, 0:d]))
    gbm = jax.nn.sigmoid(_dot(hb, wg_ref[:, d:2 * d]))
    merged = ga * _dot(ya_ref[0], pa_ref[...]) + gbm * _dot(yb_ref[0], pb_ref[...])
    x1 = x + g1_ref[0] * _dot(merged.astype(BF16), wo_ref[...])
    x1_ref[0] = x1
    h2 = _rms_mod(x1, n2_ref[...], sc2_ref[0], sh2_ref[0])
    for sl in range(d // LANES):
        h2_ref[:, sl, :] = h2[:, sl * LANES:(sl + 1) * LANES]

    lane = lax.broadcasted_iota(I32, (tm, LANES), 1).astype(F32)
    logits = _dot(h2.astype(BF16), wr_ref[...]) + br_ref[...]
    logits = jnp.where(lane < N_EXPERTS, logits, -jnp.inf)
    vals, idxs = [], []
    hot = jnp.zeros((tm, LANES), F32)
    for _ in range(TOP_K):
        mx = jnp.max(logits, axis=1, keepdims=True)
        ix = jnp.min(jnp.where(logits == mx, lane, float(LANES)), axis=1, keepdims=True)
        pick = lane == ix
        hot = jnp.where(pick, 1.0, hot)
        logits = jnp.where(pick, -jnp.inf, logits)
        vals.append(mx)
        idxs.append(ix)
    es = [jnp.exp(vv - vals[0]) for vv in vals]
    den = es[0] + es[1] + es[2] + es[3]

    prefix = _dot(tri_ref[...], hot.astype(BF16)) + carry[0:1, :]
    carry[...] = jnp.broadcast_to(prefix[tm - 1:tm, :] + hot[tm - 1:tm, :], carry.shape)
    cnt_ref[...] = carry[...]
    meta = jnp.zeros((tm, LANES), F32)
    for kk in range(TOP_K):
        rank = jnp.sum(jnp.where(lane == idxs[kk], prefix, 0.0), axis=1, keepdims=True)
        meta = jnp.where(lane == float(kk), idxs[kk], meta)
        meta = jnp.where(lane == float(TOP_K + kk), rank, meta)
        meta = jnp.where(lane == float(2 * TOP_K + kk), es[kk] / den, meta)
    meta_ref[...] = meta


def _merge(x, ya, yb, mods, n1, n2, w_gate, pa, pb, wo, wr, br, tri, *, tm):
    b, s, d = x.shape
    t = b * s
    sh1, sc1, g1, sh2, sc2 = mods
    nst = s // tm
    row = lambda width: pl.BlockSpec((1, tm, width), lambda bi, si: (bi, si, 0))
    mod = pl.BlockSpec((1, 1, d), lambda bi, si: (bi, 0, 0))
    full = lambda a: pl.BlockSpec(a.shape, lambda bi, si: (0,) * a.ndim)
    outs = [jax.ShapeDtypeStruct((b, s, d), F32),
            jax.ShapeDtypeStruct((t, d // LANES, LANES), F32),
            jax.ShapeDtypeStruct((t, LANES), F32),
            jax.ShapeDtypeStruct((SUBLANES, LANES), F32)]
    return pl.pallas_call(
        functools.partial(_merge_kernel, tm=tm, d=d),
        out_shape=outs,
        grid=(b, nst),
        in_specs=[row(d), row(ya.shape[2]), row(yb.shape[2]), mod, mod, mod, mod, mod, full(n1), full(n2),
                  full(w_gate), full(pa), full(pb), full(wo), full(wr), full(br), full(tri)],
        out_specs=[row(d),
                   pl.BlockSpec((tm, d // LANES, LANES), lambda bi, si: (bi * nst + si, 0, 0)),
                   pl.BlockSpec((tm, LANES), lambda bi, si: (bi * nst + si, 0)),
                   pl.BlockSpec((SUBLANES, LANES), lambda bi, si: (0, 0))],
        scratch_shapes=[pltpu.VMEM((SUBLANES, LANES), F32)],
        compiler_params=_params(("arbitrary", "arbitrary")),
        name="merge",
    )(x, ya, yb, sh1, sc1, g1, sh2, sc2, n1, n2, w_gate, pa, pb, wo, wr, br, tri)


def _plan_kernel(meta_ref, cnt_ref, tri_ref, dest_ref, be_ref, *, tm, nbp):
    lane_i = lax.broadcasted_iota(I32, (tm, LANES), 1)
    lane = lane_i.astype(F32)
    nblk = jnp.floor((cnt_ref[...] + float(EXPERT_BLOCK - 1)) * (1.0 / EXPERT_BLOCK))
    pend = _dot(nblk.astype(BF16), tri_ref[...])
    pstart = ((pend - nblk) * float(EXPERT_BLOCK))[0:1, :]
    meta = meta_ref[...]
    dest = jnp.zeros((tm, LANES), F32)
    for kk in range(TOP_K):
        ix = meta[:, kk:kk + 1]
        rank = meta[:, TOP_K + kk:TOP_K + kk + 1]
        base = jnp.sum(jnp.where(lane == ix, pstart, 0.0), axis=1, keepdims=True)
        dest = jnp.where(lane == float(kk), base + rank, dest)
    dest_ref[...] = dest.astype(I32)

    blk = lax.broadcasted_iota(I32, (nbp, LANES), 0).astype(F32)
    lane_b = lax.broadcasted_iota(I32, (nbp, LANES), 1)
    ends = jnp.where(lane_b < N_EXPERTS, pend[0:1, :], float(2 ** 30))
    e_of = jnp.sum(jnp.where(ends <= blk, 1.0, 0.0), axis=1, keepdims=True)
    e_of = jnp.minimum(e_of, float(N_EXPERTS - 1))
    used = jnp.sum(jnp.where(lane_b == N_EXPERTS - 1, pend[0:1, :], 0.0), axis=1, keepdims=True)
    be = jnp.where(lane_b == 0, jnp.broadcast_to(e_of, (nbp, LANES)), jnp.broadcast_to(used, (nbp, LANES)))
    be_ref[...] = be.astype(I32)


def _plan(meta, counts, tri_e, *, tm, nbp):
    t = meta.shape[0]
    return pl.pallas_call(
        functools.partial(_plan_kernel, tm=tm, nbp=nbp),
        out_shape=[jax.ShapeDtypeStruct((t, LANES), I32), jax.ShapeDtypeStruct((nbp, LANES), I32)],
        grid=(t // tm,),
        in_specs=[pl.BlockSpec((tm, LANES), lambda i: (i, 0)),
                  pl.BlockSpec((SUBLANES, LANES), lambda i: (0, 0)),
                  pl.BlockSpec((LANES, LANES), lambda i: (0, 0))],
        out_specs=[pl.BlockSpec((tm, LANES), lambda i: (i, 0)),
                   pl.BlockSpec((nbp, LANES), lambda i: (0, 0))],
        compiler_params=_params(("arbitrary",)),
        name="plan",
    )(meta, counts, tri_e)


def _dispatch_kernel(dest_ref, h_ref, xs_in_ref, xs_ref, sem, *, tb):
    del xs_in_ref
    base = pl.program_id(0) * tb

    def row_copy(t, kk):
        return pltpu.make_async_copy(h_ref.at[base + t], xs_ref.at[dest_ref[t * TOP_K + kk]], sem)

    def issue(t, c):
        for kk in range(TOP_K):
            row_copy(t, kk).start()
        return c

    lax.fori_loop(0, tb, issue, 0)

    def drain(t, c):
        for kk in range(TOP_K):
            row_copy(t, kk).wait()
        return c

    lax.fori_loop(0, tb, drain, 0)


def _dispatch(dest_flat, h2, xs_zero, *, tb):
    t = h2.shape[0]
    return pl.pallas_call(
        functools.partial(_dispatch_kernel, tb=tb),
        out_shape=jax.ShapeDtypeStruct(xs_zero.shape, xs_zero.dtype),
        grid=(t // tb,),
        in_specs=[pl.BlockSpec((tb * TOP_K,), lambda i: (i,), memory_space=pltpu.SMEM),
                  pl.BlockSpec(memory_space=pl.ANY),
                  pl.BlockSpec(memory_space=pl.ANY)],
        out_specs=pl.BlockSpec(memory_space=pl.ANY),
        scratch_shapes=[pltpu.SemaphoreType.DMA(())],
        input_output_aliases={2: 0},
        compiler_params=_params(("arbitrary",)),
        name="dispatch",
    )(dest_flat, h2, xs_zero)


def _ffn_kernel(be_ref, used_ref, xs_ref, wgu_ref, bgu_ref, wd_ref, bd_ref, ys_ref, wgu_bf, wd_bf, last_e,
                *, d, dff):
    i = pl.program_id(0)
    e = be_ref[i]

    @pl.when(i == 0)
    def _():
        last_e[0] = -1

    @pl.when(i < used_ref[0])
    def _():
        @pl.when(last_e[0] != e)
        def _():
            wgu_bf[...] = wgu_ref[0].astype(BF16)
            wd_bf[...] = wd_ref[0].astype(BF16)
            last_e[0] = e

        xb = jnp.concatenate([xs_ref[:, sl, :] for sl in range(d // LANES)], axis=1).astype(BF16)
        gu = _dot(xb, wgu_bf[...]) + bgu_ref[0]
        x_glu = jnp.minimum(gu[:, 0:dff], SWIGLU_LIMIT)
        x_lin = jnp.clip(gu[:, dff:2 * dff], -SWIGLU_LIMIT, SWIGLU_LIMIT)
        act = x_glu * jax.nn.sigmoid(SWIGLU_ALPHA * x_glu) * (x_lin + 1.0)
        y = _dot(act.astype(BF16), wd_bf[...]) + bd_ref[0]
        for sl in range(d // LANES):
            ys_ref[:, sl, :] = y[:, sl * LANES:(sl + 1) * LANES]

    @pl.when(i >= used_ref[0])
    def _():
        ys_ref[...] = jnp.zeros(ys_ref.shape, F32)


def _ffn(block_e, used, xs, w_gate_up, b_gate_up, w_down, b_down):
    cap, nsl, _ = xs.shape
    ne, d, dff2 = w_gate_up.shape
    dff = dff2 // 2
    nblocks = cap // EXPERT_BLOCK
    grid_spec = pltpu.PrefetchScalarGridSpec(
        num_scalar_prefetch=2,
        grid=(nblocks,),
        in_specs=[pl.BlockSpec((EXPERT_BLOCK, nsl, LANES), lambda i, be, us: (i, 0, 0)),
                  pl.BlockSpec((1, d, dff2), lambda i, be, us: (be[i], 0, 0)),
                  pl.BlockSpec((1, 1, dff2), lambda i, be, us: (be[i], 0, 0)),
                  pl.BlockSpec((1, dff, d), lambda i, be, us: (be[i], 0, 0)),
                  pl.BlockSpec((1, 1, d), lambda i, be, us: (be[i], 0, 0))],
        out_specs=pl.BlockSpec((EXPERT_BLOCK, nsl, LANES), lambda i, be, us: (i, 0, 0)),
        scratch_shapes=[pltpu.VMEM((d, dff2), BF16), pltpu.VMEM((dff, d), BF16), pltpu.SMEM((1,), I32)])
    return pl.pallas_call(
        functools.partial(_ffn_kernel, d=d, dff=dff),
        out_shape=jax.ShapeDtypeStruct(xs.shape, F32),
        grid_spec=grid_spec,
        compiler_params=_params(("arbitrary",)),
        name="ffn",
    )(block_e, used, xs, w_gate_up, b_gate_up.reshape(ne, 1, dff2), w_down, b_down.reshape(ne, 1, d))


def _combine_kernel(dest_ref, ys_ref, x1_ref, g2_ref, meta_ref, o_ref, buf, sem, *, tb, d):
    def row_copy(t, kk):
        return pltpu.make_async_copy(ys_ref.at[dest_ref[t * TOP_K + kk]], buf.at[kk, t], sem)

    def issue(t, c):
        for kk in range(TOP_K):
            row_copy(t, kk).start()
        return c

    lax.fori_loop(0, tb, issue, 0)

    def drain(t, c):
        for kk in range(TOP_K):
            row_copy(t, kk).wait()
        return c

    lax.fori_loop(0, tb, drain, 0)

    meta = meta_ref[...]
    moe = None
    for kk in range(TOP_K):
        yk = jnp.concatenate([buf[kk, :, sl, :] for sl in range(d // LANES)], axis=1)
        term = yk * meta[:, 2 * TOP_K + kk:2 * TOP_K + kk + 1]
        moe = term if moe is None else moe + term
    o_ref[0] = x1_ref[0] + g2_ref[0] * moe


def _combine(dest_flat, ys, x1, g2, meta, *, tb):
    b, s, d = x1.shape
    nst = s // tb
    nsl = d // LANES
    return pl.pallas_call(
        functools.partial(_combine_kernel, tb=tb, d=d),
        out_shape=jax.ShapeDtypeStruct((b, s, d), F32),
        grid=(b, nst),
        in_specs=[pl.BlockSpec((tb * TOP_K,), lambda bi, si: (bi * nst + si,), memory_space=pltpu.SMEM),
                  pl.BlockSpec(memory_space=pl.ANY),
                  pl.BlockSpec((1, tb, d), lambda bi, si: (bi, si, 0)),
                  pl.BlockSpec((1, 1, d), lambda bi, si: (bi, 0, 0)),
                  pl.BlockSpec((tb, LANES), lambda bi, si: (bi * nst + si, 0))],
        out_specs=pl.BlockSpec((1, tb, d), lambda bi, si: (bi, si, 0)),
        scratch_shapes=[pltpu.VMEM((TOP_K, tb, nsl, LANES), F32), pltpu.SemaphoreType.DMA(())],
        compiler_params=_params(("arbitrary", "arbitrary")),
        name="combine",
    )(dest_flat, ys, x1, g2, meta)


def _tri(n, strict, upper):
    r = lax.broadcasted_iota(I32, (n, n), 0)
    c = lax.broadcasted_iota(I32, (n, n), 1)
    if upper:
        m = (r < c) if strict else (r <= c)
    else:
        m = (c < r) if strict else (c <= r)
    return m.astype(BF16)


def _layer(x, c_pad, w_ada, b_ada, norm1_g, w_in, conv_w, q_norm_g, k_norm_g, w_conv_out, w_attn_out, w_o,
           norm2_g, w_router, b_router, w_gate_up, b_gate_up, w_down, b_down):
    b, s, d = x.shape
    t = b * s
    cw = conv_w.shape[1]
    aw = N_HEADS * HEAD_DIM
    tm = min(512, s)
    qb = min(IDX_TOPK, s // 4)
    tb = 256

    mod = _ada(c_pad, w_ada, b_ada)[:b]
    sh1, sc1, g1, sh2, sc2, g2 = [m.reshape(b, 1, d) for m in jnp.split(mod, 6, axis=-1)]

    o = 3 * cw + 3 * aw
    w_qi = w_in[:, o:o + IDX_HEADS * IDX_DIM].reshape(d, IDX_HEADS, IDX_DIM)
    w_qi = jnp.pad(w_qi, ((0, 0), (0, 0), (0, LANES - IDX_DIM))).reshape(d, IDX_HEADS * LANES)
    o2 = o + IDX_HEADS * IDX_DIM
    w_ki = jnp.pad(w_in[:, o2:o2 + IDX_DIM], ((0, 0), (0, LANES - IDX_DIM)))
    o3 = o2 + IDX_DIM
    w_wi = jnp.pad(w_in[:, o3:o3 + IDX_HEADS], ((0, 0), (0, LANES - IDX_HEADS)))
    o4 = o3 + IDX_HEADS
    w_main = jnp.concatenate([w_in[:, :o], w_qi, w_ki, w_wi], axis=1).astype(BF16)
    w_gate = w_in[:, o4:o4 + 2 * d].astype(BF16)
    seg = lax.broadcasted_iota(I32, (aw, aw), 0) // HEAD_DIM == lax.broadcasted_iota(I32, (aw, aw), 1) // HEAD_DIM
    bd = jnp.where(seg, 1.0 / HEAD_DIM, 0.0).astype(BF16)
    qg = jnp.tile(q_norm_g, N_HEADS).reshape(1, aw)
    kg = jnp.tile(k_norm_g, N_HEADS).reshape(1, aw)

    ya, q, k, v, qi, ki, wi = _inproj(x, sh1, sc1, norm1_g.reshape(1, d), w_main, conv_w, qg, kg, bd, tm=tm)
    yb = _attn(q, k, v, qi, ki, wi, _tri(qb, strict=False, upper=True), qb=qb)

    wr = jnp.pad(w_router, ((0, 0), (0, LANES - N_EXPERTS))).astype(BF16)
    br = jnp.pad(b_router, (0, LANES - N_EXPERTS)).reshape(1, LANES)
    x1, h2, meta, counts = _merge(
        x, ya, yb, (sh1, sc1, g1, sh2, sc2), norm1_g.reshape(1, d), norm2_g.reshape(1, d), w_gate,
        w_conv_out.astype(BF16), w_attn_out.astype(BF16), w_o.astype(BF16), wr, br,
        _tri(tm, strict=True, upper=False), tm=tm)

    nk = t * TOP_K
    cap = -(-nk // EXPERT_BLOCK) * EXPERT_BLOCK + N_EXPERTS * EXPERT_BLOCK
    nblocks = cap // EXPERT_BLOCK
    nbp = -(-nblocks // SUBLANES) * SUBLANES
    dest_pad, be_pad = _plan(meta, counts, _tri(LANES, strict=False, upper=True), tm=tm, nbp=nbp)
    dest_flat = dest_pad[:, :TOP_K].reshape(-1)
    block_e = be_pad[:nblocks, 0]
    used = be_pad[0:1, 1]

    xs = _dispatch(dest_flat, h2, jnp.zeros((cap, d // LANES, LANES), F32), tb=tb)
    ys = _ffn(block_e, used, xs, w_gate_up, b_gate_up, w_down, b_down)
    return _combine(dest_flat, ys, x1, g2, meta, tb=tb)


def kernel(x, c, w_ada, b_ada, norm1_g, w_in, conv_w, q_norm_g, k_norm_g, w_conv_out, w_attn_out, w_o, norm2_g,
           w_router, b_router, w_gate_up, b_gate_up, w_down, b_down):
    c_pad = jnp.pad(c, ((0, -c.shape[0] % SUBLANES), (0, 0)))
    for l in range(w_ada.shape[0]):
        x = _layer(x, c_pad, w_ada[l], b_ada[l], norm1_g[l], w_in[l], conv_w[l], q_norm_g[l], k_norm_g[l],
                   w_conv_out[l], w_attn_out[l], w_o[l], norm2_g[l], w_router[l], b_router[l],
                   w_gate_up[l], b_gate_up[l], w_down[l], b_down[l])
    return x
```

```python
import functools

import jax
import jax.numpy as jnp
from jax import lax
from jax.experimental import pallas as pl
from jax.experimental.pallas import tpu as pltpu

F32 = jnp.float32
BF16 = jnp.bfloat16
I32 = jnp.int32

LANES = 128
SUBLANES = 8

CHUNK = 64
CONV_K = 3
N_HEADS = 8
HEAD_DIM = 64
IDX_HEADS = 4
IDX_DIM = 64
IDX_TOPK = 256
N_EXPERTS = 32
TOP_K = 4
SWIGLU_LIMIT = 7.0
SWIGLU_ALPHA = 1.702
EXPERT_BLOCK = 256
EPS = 1e-6

INT_MIN = -(2 ** 31)
VMEM_LIMIT = 56 * 1024 * 1024


def _dot(a, b):
    return jnp.dot(a, b, preferred_element_type=F32)


def _dot_nt(a, b):
    return lax.dot_general(a, b, (((1,), (1,)), ((), ())), preferred_element_type=F32)


def _params(sem, vmem=VMEM_LIMIT):
    return pltpu.CompilerParams(dimension_semantics=sem, vmem_limit_bytes=vmem)


def _ada_kernel(c_ref, w_ref, b_ref, o_ref):
    c = c_ref[...]
    s = c * jax.nn.sigmoid(c)
    o_ref[...] = _dot(s.astype(BF16), w_ref[...].astype(BF16)) + b_ref[...]


def _ada(c_pad, w_ada, b_ada):
    rows, d = c_pad.shape
    n = w_ada.shape[1]
    tn = n // 6
    return pl.pallas_call(
        _ada_kernel,
        out_shape=jax.ShapeDtypeStruct((rows, n), F32),
        grid=(n // tn,),
        in_specs=[pl.BlockSpec((rows, d), lambda j: (0, 0)),
                  pl.BlockSpec((d, tn), lambda j: (0, j)),
                  pl.BlockSpec((1, tn), lambda j: (0, j))],
        out_specs=pl.BlockSpec((rows, tn), lambda j: (0, j)),
        compiler_params=_params(("arbitrary",)),
        name="ada",
    )(c_pad, w_ada, b_ada.reshape(1, n))


def _rms_mod(x, g, sc, sh):
    ms = jnp.mean(x * x, axis=-1, keepdims=True)
    y = x * lax.rsqrt(ms + EPS) * g
    return y * (1.0 + sc) + sh


def _head_rms(t, bd, g):
    sq = t * t
    hi = sq.astype(BF16)
    lo = (sq - hi.astype(F32)).astype(BF16)
    ms = _dot(hi, bd) + _dot(lo, bd)
    return t * lax.rsqrt(ms + EPS) * g


def _inproj_kernel(x_ref, sh_ref, sc_ref, g_ref, w_ref, cw_ref, qg_ref, kg_ref, bd_ref,
                   ya_ref, q_ref, k_ref, v_ref, qi_ref, ki_ref, wi_ref, ubuf, *, tm, cw, aw):
    s = pl.program_id(1)
    hb = _rms_mod(x_ref[0], g_ref[...], sc_ref[0], sh_ref[0]).astype(BF16)

    zc = _dot(hb, w_ref[:, 0:3 * cw])
    xin, gb, gc = zc[:, 0:cw], zc[:, cw:2 * cw], zc[:, 2 * cw:3 * cw]
    u = gc * xin

    @pl.when(s == 0)
    def _():
        ubuf[0:SUBLANES, :] = jnp.zeros((SUBLANES, cw), F32)

    ubuf[SUBLANES:SUBLANES + tm, :] = u
    u1 = ubuf[SUBLANES - 1:SUBLANES - 1 + tm, :]
    u2 = ubuf[SUBLANES - 2:SUBLANES - 2 + tm, :]
    conv = cw_ref[0:1, :] * u2 + cw_ref[1:2, :] * u1 + cw_ref[2:3, :] * u
    ya_ref[0] = (gb * conv).astype(BF16)
    ubuf[0:SUBLANES, :] = ubuf[tm:tm + SUBLANES, :]

    o = 3 * cw
    bd = bd_ref[...]
    q = _dot(hb, w_ref[:, o:o + aw])
    q_ref[0] = (_head_rms(q, bd, qg_ref[...]) * (HEAD_DIM ** -0.5)).astype(BF16)
    k = _dot(hb, w_ref[:, o + aw:o + 2 * aw])
    k_ref[0] = _head_rms(k, bd, kg_ref[...]).astype(BF16)
    v_ref[0] = _dot(hb, w_ref[:, o + 2 * aw:o + 3 * aw]).astype(BF16)

    o = o + 3 * aw
    qw = IDX_HEADS * LANES
    qi_ref[0] = _dot(hb, w_ref[:, o:o + qw]).astype(BF16)
    ki_ref[0] = _dot(hb, w_ref[:, o + qw:o + qw + LANES]).astype(BF16)
    wi_ref[0] = _dot(hb, w_ref[:, o + qw + LANES:o + qw + 2 * LANES]) * ((IDX_HEADS * IDX_DIM) ** -0.5)


def _inproj(x, sh1, sc1, g1n, w_main, conv_w, qg, kg, bd, *, tm):
    b, s, d = x.shape
    cw = conv_w.shape[1]
    aw = N_HEADS * HEAD_DIM
    nw = w_main.shape[1]
    qw = IDX_HEADS * LANES
    row = lambda width: pl.BlockSpec((1, tm, width), lambda bi, si: (bi, si, 0))
    mod = pl.BlockSpec((1, 1, d), lambda bi, si: (bi, 0, 0))
    full = lambda a: pl.BlockSpec(a.shape, lambda bi, si: (0,) * a.ndim)
    outs = [jax.ShapeDtypeStruct((b, s, cw), BF16)] + [jax.ShapeDtypeStruct((b, s, aw), BF16)] * 3 + [
        jax.ShapeDtypeStruct((b, s, qw), BF16), jax.ShapeDtypeStruct((b, s, LANES), BF16),
        jax.ShapeDtypeStruct((b, s, LANES), F32)]
    return pl.pallas_call(
        functools.partial(_inproj_kernel, tm=tm, cw=cw, aw=aw),
        out_shape=outs,
        grid=(b, s // tm),
        in_specs=[row(d), mod, mod, full(g1n), full(w_main), full(conv_w), full(qg), full(kg), full(bd)],
        out_specs=[row(cw), row(aw), row(aw), row(aw), row(qw), row(LANES), row(LANES)],
        scratch_shapes=[pltpu.VMEM((tm + SUBLANES, cw), F32)],
        compiler_params=_params(("arbitrary", "arbitrary")),
        name="inproj",
    )(x, sh1, sc1, g1n, w_main, conv_w, qg, kg, bd)


def _sort_key(score):
    bits = pltpu.bitcast(score, I32)
    return jnp.where(bits < 0, INT_MIN - bits, bits)


def _attn_kernel(q_ref, k_ref, v_ref, qi_ref, ki_ref, wi_ref, ut_ref, o_ref,
                 key_scr, t_scr, need_scr, m_scr, l_scr, acc_scr, *, qb, topk, rb):
    i = pl.program_id(1)
    kt = qb
    ncol = kt // LANES
    npair = N_HEADS // 2
    neg_inf = -jnp.inf

    qi = qi_ref[0]
    q4 = jnp.concatenate([qi[:, h * LANES:(h + 1) * LANES] for h in range(IDX_HEADS)], axis=0)
    wi = wi_ref[0]
    wrep = [jnp.broadcast_to(wi[:, h:h + 1], (qb, kt)) for h in range(IDX_HEADS)]
    row_chunk_end = ((lax.broadcasted_iota(I32, (qb, kt), 0) + i * qb) // CHUNK + 1) * CHUNK

    def score_tile(j, carry):
        off = pl.multiple_of(j * kt, kt)
        dots = _dot_nt(q4, ki_ref[0, pl.ds(off, kt), :])
        sc = wrep[0] * jnp.maximum(dots[0:qb], 0.0)
        for h in range(1, IDX_HEADS):
            sc = sc + wrep[h] * jnp.maximum(dots[h * qb:(h + 1) * qb], 0.0)
        key = _sort_key(sc)
        kpos = lax.broadcasted_iota(I32, (qb, kt), 1) + j * kt
        key = jnp.where(kpos < row_chunk_end, key, INT_MIN)
        key_scr[:, pl.ds(off, kt)] = key
        return carry

    lax.fori_loop(0, i + 1, score_tile, 0)
    key_scr[:, pl.ds(pl.multiple_of((i + 1) * kt, kt), kt)] = jnp.full((qb, kt), INT_MIN, I32)

    def lane_counts(r0, cand):
        def body(jj, acc):
            off = pl.multiple_of(jj * (2 * kt), 2 * kt)
            for c in range(2 * ncol):
                kk = key_scr[r0:r0 + rb, pl.ds(off + c * LANES, LANES)]
                acc = acc + jnp.where(kk >= cand, 1.0, 0.0)
            return acc
        return lax.fori_loop(0, (i + 2) // 2, body, jnp.zeros((rb, LANES), F32))

    def row_total(acc):
        return jnp.broadcast_to(jnp.sum(acc, axis=1, keepdims=True), (rb, LANES))

    @pl.when(i == 0)
    def _():
        t_scr[...] = jnp.full((qb, LANES), INT_MIN + 1, I32)
        need_scr[...] = jnp.full((qb, LANES), float(topk), F32)

    @pl.when(i > 0)
    def _():
        kf = float(topk)
        blocks = [r * rb for r in range(qb // rb)]
        zero = jnp.zeros((rb, LANES), I32)
        accs = [lane_counts(r0, zero) for r0 in blocks]
        for r0, acc in zip(blocks, accs):
            t_scr[r0:r0 + rb, :] = jnp.where(row_total(acc) >= kf, zero, jnp.full((rb, LANES), INT_MIN, I32))

        def bit_step(it, carry):
            inc = lax.shift_left(jnp.int32(1), jnp.int32(30) - it)
            accs = [lane_counts(r0, t_scr[r0:r0 + rb, :] + inc) for r0 in blocks]
            for r0, acc in zip(blocks, accs):
                res = t_scr[r0:r0 + rb, :]
                t_scr[r0:r0 + rb, :] = jnp.where(row_total(acc) >= kf, res + inc, res)
            return carry

        lax.fori_loop(0, 31, bit_step, 0)
        accs = [lane_counts(r0, t_scr[r0:r0 + rb, :] + 1) for r0 in blocks]
        for r0, acc in zip(blocks, accs):
            need_scr[r0:r0 + rb, :] = kf - row_total(acc)

    m_scr[...] = jnp.full(m_scr.shape, neg_inf, F32)
    l_scr[...] = jnp.zeros(l_scr.shape, F32)
    acc_scr[...] = jnp.zeros(acc_scr.shape, F32)
    lane = lax.broadcasted_iota(I32, (qb, LANES), 1)
    low_half = lane < HEAD_DIM
    q = q_ref[0]
    zero_q = jnp.zeros((qb, LANES), BF16)
    qm = []
    for h in range(N_HEADS):
        qp = q[:, (h // 2) * LANES:(h // 2 + 1) * LANES]
        qm.append(jnp.where(low_half if h % 2 == 0 else jnp.logical_not(low_half), qp, zero_q))
    thr = jnp.concatenate([t_scr[...]] * ncol, axis=1)
    need = jnp.concatenate([need_scr[...]] * ncol, axis=1)
    ut = ut_ref[...]

    def attn_tile(j, run):
        off = pl.multiple_of(j * kt, kt)
        key = key_scr[:, pl.ds(off, kt)]
        eq = key == thr
        cnt = _dot(jnp.where(eq, 1.0, 0.0).astype(BF16), ut) + run
        sel = jnp.logical_or(key > thr, jnp.logical_and(eq, cnt <= need))
        bias = jnp.where(sel, 0.0, neg_inf)
        for h in range(N_HEADS):
            p2 = h // 2
            kp = k_ref[0, pl.ds(off, kt), p2 * LANES:(p2 + 1) * LANES]
            vp = v_ref[0, pl.ds(off, kt), p2 * LANES:(p2 + 1) * LANES]
            s = _dot_nt(qm[h], kp) + bias
            m_old = m_scr[h]
            m_new = jnp.maximum(m_old, jnp.broadcast_to(jnp.max(s, axis=1, keepdims=True), (qb, LANES)))
            m_safe = jnp.where(m_new == neg_inf, 0.0, m_new)
            alpha = jnp.exp(m_old - m_safe)
            p = jnp.exp(s - jnp.concatenate([m_safe] * ncol, axis=1))
            l_scr[h] = alpha * l_scr[h] + jnp.broadcast_to(jnp.sum(p, axis=1, keepdims=True), (qb, LANES))
            acc_scr[h] = alpha * acc_scr[h] + _dot(p.astype(BF16), vp)
            m_scr[h] = m_new
        return jnp.broadcast_to(cnt[:, kt - 1:kt], (qb, kt))

    lax.fori_loop(0, i + 1, attn_tile, jnp.zeros((qb, kt), F32))

    for p2 in range(npair):
        lo = acc_scr[2 * p2] / l_scr[2 * p2]
        hi = acc_scr[2 * p2 + 1] / l_scr[2 * p2 + 1]
        o_ref[0, :, p2 * LANES:(p2 + 1) * LANES] = jnp.where(low_half, lo, hi).astype(BF16)


def _attn(q, k, v, qi, ki, wi, ut, *, qb):
    b, s, aw = q.shape
    topk = min(IDX_TOPK, s // 4)
    assert topk == qb and s % qb == 0 and qb % CHUNK == 0
    rowq = lambda width: pl.BlockSpec((1, qb, width), lambda bi, i: (bi, i, 0))
    allk = lambda width: pl.BlockSpec((1, s, width), lambda bi, i: (bi, 0, 0))
    return pl.pallas_call(
        functools.partial(_attn_kernel, qb=qb, topk=topk, rb=128),
        out_shape=jax.ShapeDtypeStruct((b, s, aw), BF16),
        grid=(b, s // qb),
        in_specs=[rowq(aw), allk(aw), allk(aw), rowq(qi.shape[2]), allk(LANES), rowq(LANES),
                  pl.BlockSpec(ut.shape, lambda bi, i: (0, 0))],
        out_specs=rowq(aw),
        scratch_shapes=[pltpu.VMEM((qb, s + qb), I32), pltpu.VMEM((qb, LANES), I32), pltpu.VMEM((qb, LANES), F32),
                        pltpu.VMEM((N_HEADS, qb, LANES), F32), pltpu.VMEM((N_HEADS, qb, LANES), F32),
                        pltpu.VMEM((N_HEADS, qb, LANES), F32)],
        compiler_params=_params(("arbitrary", "arbitrary")),
        name="attn",
    )(q, k, v, qi, ki, wi, ut)


def _merge_kernel(x_ref, ya_ref, yb_ref, sh1_ref, sc1_ref, g1_ref, sh2_ref, sc2_ref, n1_ref, n2_ref,
                  wg_ref, pa_ref, pb_ref, wo_ref, wr_ref, br_ref, tri_ref,
                  x1_ref, h2_ref, meta_ref, cnt_ref, carry, *, tm, d):
    first = jnp.logical_and(pl.program_id(0) == 0, pl.program_id(1) == 0)

    @pl.when(first)
    def _():
        carry[...] = jnp.zeros(carry.shape, F32)

    x = x_ref[0]
    hb = _rms_mod(x, n1_ref[...], sc1_ref[0], sh1_ref[0]).astype(BF16)
    ga = jax.nn.sigmoid(_dot(hb, wg_ref[:, 0:d]))
    gbm = jax.nn.sigmoid(_dot(hb, wg_ref[:, d:2 * d]))
    merged = ga * _dot(ya_ref[0], pa_ref[...]) + gbm * _dot(yb_ref[0], pb_ref[...])
    x1 = x + g1_ref[0] * _dot(merged.astype(BF16), wo_ref[...])
    x1_ref[0] = x1
    h2 = _rms_mod(x1, n2_ref[...], sc2_ref[0], sh2_ref[0])
    for sl in range(d // LANES):
        h2_ref[:, sl, :] = h2[:, sl * LANES:(sl + 1) * LANES]

    lane = lax.broadcasted_iota(I32, (tm, LANES), 1).astype(F32)
    logits = _dot(h2.astype(BF16), wr_ref[...]) + br_ref[...]
    logits = jnp.where(lane < N_EXPERTS, logits, -jnp.inf)
    vals, idxs = [], []
    hot = jnp.zeros((tm, LANES), F32)
    for _ in range(TOP_K):
        mx = jnp.max(logits, axis=1, keepdims=True)
        ix = jnp.min(jnp.where(logits == mx, lane, float(LANES)), axis=1, keepdims=True)
        pick = lane == ix
        hot = jnp.where(pick, 1.0, hot)
        logits = jnp.where(pick, -jnp.inf, logits)
        vals.append(mx)
        idxs.append(ix)
    es = [jnp.exp(vv - vals[0]) for vv in vals]
    den = es[0] + es[1] + es[2] + es[3]

    prefix = _dot(tri_ref[...], hot.astype(BF16)) + carry[0:1, :]
    carry[...] = jnp.broadcast_to(prefix[tm - 1:tm, :] + hot[tm - 1:tm, :], carry.shape)
    cnt_ref[...] = carry[...]
    meta = jnp.zeros((tm, LANES), F32)
    for kk in range(TOP_K):
        rank = jnp.sum(jnp.where(lane == idxs[kk], prefix, 0.0), axis=1, keepdims=True)
        meta = jnp.where(lane == float(kk), idxs[kk], meta)
        meta = jnp.where(lane == float(TOP_K + kk), rank, meta)
        meta = jnp.where(lane == float(2 * TOP_K + kk), es[kk] / den, meta)
    meta_ref[...] = meta


def _merge(x, ya, yb, mods, n1, n2, w_gate, pa, pb, wo, wr, br, tri, *, tm):
    b, s, d = x.shape
    t = b * s
    sh1, sc1, g1, sh2, sc2 = mods
    nst = s // tm
    row = lambda width: pl.BlockSpec((1, tm, width), lambda bi, si: (bi, si, 0))
    mod = pl.BlockSpec((1, 1, d), lambda bi, si: (bi, 0, 0))
    full = lambda a: pl.BlockSpec(a.shape, lambda bi, si: (0,) * a.ndim)
    outs = [jax.ShapeDtypeStruct((b, s, d), F32),
            jax.ShapeDtypeStruct((t, d // LANES, LANES), F32),
            jax.ShapeDtypeStruct((t, LANES), F32),
            jax.ShapeDtypeStruct((SUBLANES, LANES), F32)]
    return pl.pallas_call(
        functools.partial(_merge_kernel, tm=tm, d=d),
        out_shape=outs,
        grid=(b, nst),
        in_specs=[row(d), row(ya.shape[2]), row(yb.shape[2]), mod, mod, mod, mod, mod, full(n1), full(n2),
                  full(w_gate), full(pa), full(pb), full(wo), full(wr), full(br), full(tri)],
        out_specs=[row(d),
                   pl.BlockSpec((tm, d // LANES, LANES), lambda bi, si: (bi * nst + si, 0, 0)),
                   pl.BlockSpec((tm, LANES), lambda bi, si: (bi * nst + si, 0)),
                   pl.BlockSpec((SUBLANES, LANES), lambda bi, si: (0, 0))],
        scratch_shapes=[pltpu.VMEM((SUBLANES, LANES), F32)],
        compiler_params=_params(("arbitrary", "arbitrary")),
        name="merge",
    )(x, ya, yb, sh1, sc1, g1, sh2, sc2, n1, n2, w_gate, pa, pb, wo, wr, br, tri)


def _plan_kernel(meta_ref, cnt_ref, tri_ref, dest_ref, be_ref, *, tm, nbp):
    lane_i = lax.broadcasted_iota(I32, (tm, LANES), 1)
    lane = lane_i.astype(F32)
    nblk = jnp.floor((cnt_ref[...] + float(EXPERT_BLOCK - 1)) * (1.0 / EXPERT_BLOCK))
    pend = _dot(nblk.astype(BF16), tri_ref[...])
    pstart = ((pend - nblk) * float(EXPERT_BLOCK))[0:1, :]
    meta = meta_ref[...]
    dest = jnp.zeros((tm, LANES), F32)
    for kk in range(TOP_K):
        ix = meta[:, kk:kk + 1]
        rank = meta[:, TOP_K + kk:TOP_K + kk + 1]
        base = jnp.sum(jnp.where(lane == ix, pstart, 0.0), axis=1, keepdims=True)
        dest = jnp.where(lane == float(kk), base + rank, dest)
    dest_ref[...] = dest.astype(I32)

    blk = lax.broadcasted_iota(I32, (nbp, LANES), 0).astype(F32)
    lane_b = lax.broadcasted_iota(I32, (nbp, LANES), 1)
    ends = jnp.where(lane_b < N_EXPERTS, pend[0:1, :], float(2 ** 30))
    e_of = jnp.sum(jnp.where(ends <= blk, 1.0, 0.0), axis=1, keepdims=True)
    e_of = jnp.minimum(e_of, float(N_EXPERTS - 1))
    used = jnp.sum(jnp.where(lane_b == N_EXPERTS - 1, pend[0:1, :], 0.0), axis=1, keepdims=True)
    be = jnp.where(lane_b == 0, jnp.broadcast_to(e_of, (nbp, LANES)), jnp.broadcast_to(used, (nbp, LANES)))
    be_ref[...] = be.astype(I32)


def _plan(meta, counts, tri_e, *, tm, nbp):
    t = meta.shape[0]
    return pl.pallas_call(
        functools.partial(_plan_kernel, tm=tm, nbp=nbp),
        out_shape=[jax.ShapeDtypeStruct((t, LANES), I32), jax.ShapeDtypeStruct((nbp, LANES), I32)],
        grid=(t // tm,),
        in_specs=[pl.BlockSpec((tm, LANES), lambda i: (i, 0)),
                  pl.BlockSpec((SUBLANES, LANES), lambda i: (0, 0)),
                  pl.BlockSpec((LANES, LANES), lambda i: (0, 0))],
        out_specs=[pl.BlockSpec((tm, LANES), lambda i: (i, 0)),
                   pl.BlockSpec((nbp, LANES), lambda i: (0, 0))],
        compiler_params=_params(("arbitrary",)),
        name="plan",
    )(meta, counts, tri_e)


def _dispatch_kernel(dest_ref, h_ref, xs_in_ref, xs_ref, sem, *, tb):
    del xs_in_ref

    def row_copy(t, kk):
        return pltpu.make_async_copy(h_ref.at[t], xs_ref.at[dest_ref[t * TOP_K + kk]], sem)

    def issue(t, c):
        for kk in range(TOP_K):
            row_copy(t, kk).start()
        return c

    lax.fori_loop(0, tb, issue, 0)

    def drain(t, c):
        for kk in range(TOP_K):
            row_copy(t, kk).wait()
        return c

    lax.fori_loop(0, tb, drain, 0)


def _dispatch(dest_flat, h2, xs_zero, *, tb):
    t = h2.shape[0]
    return pl.pallas_call(
        functools.partial(_dispatch_kernel, tb=tb),
        out_shape=jax.ShapeDtypeStruct(xs_zero.shape, xs_zero.dtype),
        grid=(t // tb,),
        in_specs=[pl.BlockSpec((tb * TOP_K,), lambda i: (i,), memory_space=pltpu.SMEM),
                  pl.BlockSpec((tb,) + h2.shape[1:], lambda i: (i, 0, 0)),
                  pl.BlockSpec(memory_space=pl.ANY)],
        out_specs=pl.BlockSpec(memory_space=pl.ANY),
        scratch_shapes=[pltpu.SemaphoreType.DMA(())],
        input_output_aliases={2: 0},
        compiler_params=_params(("arbitrary",)),
        name="dispatch",
    )(dest_flat, h2, xs_zero)


def _ffn_kernel(be_ref, used_ref, xs_ref, wgu_ref, bgu_ref, wd_ref, bd_ref, ys_ref, wgu_bf, wd_bf, last_e,
                *, d, dff):
    i = pl.program_id(0)
    e = be_ref[i]

    @pl.when(i == 0)
    def _():
        last_e[0] = -1

    @pl.when(i < used_ref[0])
    def _():
        @pl.when(last_e[0] != e)
        def _():
            wgu_bf[...] = wgu_ref[0].astype(BF16)
            wd_bf[...] = wd_ref[0].astype(BF16)
            last_e[0] = e

        xb = jnp.concatenate([xs_ref[:, sl, :] for sl in range(d // LANES)], axis=1).astype(BF16)
        gu = _dot(xb, wgu_bf[...]) + bgu_ref[0]
        x_glu = jnp.minimum(gu[:, 0:dff], SWIGLU_LIMIT)
        x_lin = jnp.clip(gu[:, dff:2 * dff], -SWIGLU_LIMIT, SWIGLU_LIMIT)
        act = x_glu * jax.nn.sigmoid(SWIGLU_ALPHA * x_glu) * (x_lin + 1.0)
        y = _dot(act.astype(BF16), wd_bf[...]) + bd_ref[0]
        for sl in range(d // LANES):
            ys_ref[:, sl, :] = y[:, sl * LANES:(sl + 1) * LANES]

    @pl.when(i >= used_ref[0])
    def _():
        ys_ref[...] = jnp.zeros(ys_ref.shape, F32)


def _ffn(block_e, used, xs, w_gate_up, b_gate_up, w_down, b_down):
    cap, nsl, _ = xs.shape
    ne, d, dff2 = w_gate_up.shape
    dff = dff2 // 2
    nblocks = cap // EXPERT_BLOCK
    grid_spec = pltpu.PrefetchScalarGridSpec(
        num_scalar_prefetch=2,
        grid=(nblocks,),
        in_specs=[pl.BlockSpec((EXPERT_BLOCK, nsl, LANES), lambda i, be, us: (i, 0, 0)),
                  pl.BlockSpec((1, d, dff2), lambda i, be, us: (be[i], 0, 0)),
                  pl.BlockSpec((1, 1, dff2), lambda i, be, us: (be[i], 0, 0)),
                  pl.BlockSpec((1, dff, d), lambda i, be, us: (be[i], 0, 0)),
                  pl.BlockSpec((1, 1, d), lambda i, be, us: (be[i], 0, 0))],
        out_specs=pl.BlockSpec((EXPERT_BLOCK, nsl, LANES), lambda i, be, us: (i, 0, 0)),
        scratch_shapes=[pltpu.VMEM((d, dff2), BF16), pltpu.VMEM((dff, d), BF16), pltpu.SMEM((1,), I32)])
    return pl.pallas_call(
        functools.partial(_ffn_kernel, d=d, dff=dff),
        out_shape=jax.ShapeDtypeStruct(xs.shape, F32),
        grid_spec=grid_spec,
        compiler_params=_params(("arbitrary",)),
        name="ffn",
    )(block_e, used, xs, w_gate_up, b_gate_up.reshape(ne, 1, dff2), w_down, b_down.reshape(ne, 1, d))


def _combine_kernel(dest_ref, ys_ref, x1_ref, g2_ref, meta_ref, o_ref, buf, sem, *, tb, d):
    def row_copy(t, kk):
        return pltpu.make_async_copy(ys_ref.at[dest_ref[t * TOP_K + kk]], buf.at[kk, t], sem)

    def issue(t, c):
        for kk in range(TOP_K):
            row_copy(t, kk).start()
        return c

    lax.fori_loop(0, tb, issue, 0)

    def drain(t, c):
        for kk in range(TOP_K):
            row_copy(t, kk).wait()
        return c

    lax.fori_loop(0, tb, drain, 0)

    meta = meta_ref[...]
    moe = None
    for kk in range(TOP_K):
        yk = jnp.concatenate([buf[kk, :, sl, :] for sl in range(d // LANES)], axis=1)
        term = yk * meta[:, 2 * TOP_K + kk:2 * TOP_K + kk + 1]
        moe = term if moe is None else moe + term
    o_ref[0] = x1_ref[0] + g2_ref[0] * moe


def _combine(dest_flat, ys, x1, g2, meta, *, tb):
    b, s, d = x1.shape
    nst = s // tb
    nsl = d // LANES
    return pl.pallas_call(
        functools.partial(_combine_kernel, tb=tb, d=d),
        out_shape=jax.ShapeDtypeStruct((b, s, d), F32),
        grid=(b, nst),
        in_specs=[pl.BlockSpec((tb * TOP_K,), lambda bi, si: (bi * nst + si,), memory_space=pltpu.SMEM),
                  pl.BlockSpec(memory_space=pl.ANY),
                  pl.BlockSpec((1, tb, d), lambda bi, si: (bi, si, 0)),
                  pl.BlockSpec((1, 1, d), lambda bi, si: (bi, 0, 0)),
                  pl.BlockSpec((tb, LANES), lambda bi, si: (bi * nst + si, 0))],
        out_specs=pl.BlockSpec((1, tb, d), lambda bi, si: (bi, si, 0)),
        scratch_shapes=[pltpu.VMEM((TOP_K, tb, nsl, LANES), F32), pltpu.SemaphoreType.DMA(())],
        compiler_params=_params(("arbitrary", "arbitrary")),
        name="combine",
    )(dest_flat, ys, x1, g2, meta)


def _tri(n, strict, upper):
    r = lax.broadcasted_iota(I32, (n, n), 0)
    c = lax.broadcasted_iota(I32, (n, n), 1)
    if upper:
        m = (r < c) if strict else (r <= c)
    else:
        m = (c < r) if strict else (c <= r)
    return m.astype(BF16)


def _layer(x, c_pad, w_ada, b_ada, norm1_g, w_in, conv_w, q_norm_g, k_norm_g, w_conv_out, w_attn_out, w_o,
           norm2_g, w_router, b_router, w_gate_up, b_gate_up, w_down, b_down):
    b, s, d = x.shape
    t = b * s
    cw = conv_w.shape[1]
    aw = N_HEADS * HEAD_DIM
    tm = min(512, s)
    qb = min(IDX_TOPK, s // 4)
    tb = 256

    mod = _ada(c_pad, w_ada, b_ada)[:b]
    sh1, sc1, g1, sh2, sc2, g2 = [m.reshape(b, 1, d) for m in jnp.split(mod, 6, axis=-1)]

    o = 3 * cw + 3 * aw
    w_qi = w_in[:, o:o + IDX_HEADS * IDX_DIM].reshape(d, IDX_HEADS, IDX_DIM)
    w_qi = jnp.pad(w_qi, ((0, 0), (0, 0), (0, LANES - IDX_DIM))).reshape(d, IDX_HEADS * LANES)
    o2 = o + IDX_HEADS * IDX_DIM
    w_ki = jnp.pad(w_in[:, o2:o2 + IDX_DIM], ((0, 0), (0, LANES - IDX_DIM)))
    o3 = o2 + IDX_DIM
    w_wi = jnp.pad(w_in[:, o3:o3 + IDX_HEADS], ((0, 0), (0, LANES - IDX_HEADS)))
    o4 = o3 + IDX_HEADS
    w_main = jnp.concatenate([w_in[:, :o], w_qi, w_ki, w_wi], axis=1).astype(BF16)
    w_gate = w_in[:, o4:o4 + 2 * d].astype(BF16)
    seg = lax.broadcasted_iota(I32, (aw, aw), 0) // HEAD_DIM == lax.broadcasted_iota(I32, (aw, aw), 1) // HEAD_DIM
    bd = jnp.where(seg, 1.0 / HEAD_DIM, 0.0).astype(BF16)
    qg = jnp.tile(q_norm_g, N_HEADS).reshape(1, aw)
    kg = jnp.tile(k_norm_g, N_HEADS).reshape(1, aw)

    ya, q, k, v, qi, ki, wi = _inproj(x, sh1, sc1, norm1_g.reshape(1, d), w_main, conv_w, qg, kg, bd, tm=tm)
    yb = _attn(q, k, v, qi, ki, wi, _tri(qb, strict=False, upper=True), qb=qb)

    wr = jnp.pad(w_router, ((0, 0), (0, LANES - N_EXPERTS))).astype(BF16)
    br = jnp.pad(b_router, (0, LANES - N_EXPERTS)).reshape(1, LANES)
    x1, h2, meta, counts = _merge(
        x, ya, yb, (sh1, sc1, g1, sh2, sc2), norm1_g.reshape(1, d), norm2_g.reshape(1, d), w_gate,
        w_conv_out.astype(BF16), w_attn_out.astype(BF16), w_o.astype(BF16), wr, br,
        _tri(tm, strict=True, upper=False), tm=tm)

    nk = t * TOP_K
    cap = -(-nk // EXPERT_BLOCK) * EXPERT_BLOCK + N_EXPERTS * EXPERT_BLOCK
    nblocks = cap // EXPERT_BLOCK
    nbp = -(-nblocks // SUBLANES) * SUBLANES
    dest_pad, be_pad = _plan(meta, counts, _tri(LANES, strict=False, upper=True), tm=tm, nbp=nbp)
    dest_flat = dest_pad[:, :TOP_K].reshape(-1)
    block_e = be_pad[:nblocks, 0]
    used = be_pad[0:1, 1]

    xs = _dispatch(dest_flat, h2, jnp.zeros((cap, d // LANES, LANES), F32), tb=tb)
    ys = _ffn(block_e, used, xs, w_gate_up, b_gate_up, w_down, b_down)
    return _combine(dest_flat, ys, x1, g2, meta, tb=tb)


def kernel(x, c, w_ada, b_ada, norm1_g, w_in, conv_w, q_norm_g, k_norm_g, w_conv_out, w_attn_out, w_o, norm2_g,
           w_router, b_router, w_gate_up, b_gate_up, w_down, b_down):
    c_pad = jnp.pad(c, ((0, -c.shape[0] % SUBLANES), (0, 0)))
    for l in range(w_ada.shape[0]):
        x = _layer(x, c_pad, w_ada[l], b_ada[l], norm1_g[l], w_in[l], conv_w[l], q_norm_g[l], k_norm_g[l],
                   w_conv_out[l], w_attn_out[l], w_o[l], norm2_g[l], w_router[l], b_router[l],
                   w_gate_up[l], b_gate_up[l], w_down[l], b_down[l])
    return x
```

```python
import functools

import jax
import jax.numpy as jnp
from jax import lax
from jax.experimental import pallas as pl
from jax.experimental.pallas import tpu as pltpu

F32 = jnp.float32
BF16 = jnp.bfloat16
I32 = jnp.int32
I16 = jnp.int16

LANES = 128
SUBLANES = 8

CHUNK = 64
CONV_K = 3
N_HEADS = 8
HEAD_DIM = 64
IDX_HEADS = 4
IDX_DIM = 64
IDX_TOPK = 256
N_EXPERTS = 32
TOP_K = 4
SWIGLU_LIMIT = 7.0
SWIGLU_ALPHA = 1.702
EXPERT_BLOCK = 256
EPS = 1e-6

INT_MIN = -(2 ** 31)
I16_MIN = -(2 ** 15)
VMEM_LIMIT = 56 * 1024 * 1024


def _dot(a, b):
    return jnp.dot(a, b, preferred_element_type=F32)


def _dot_nt(a, b):
    return lax.dot_general(a, b, (((1,), (1,)), ((), ())), preferred_element_type=F32)


def _params(sem, vmem=VMEM_LIMIT):
    return pltpu.CompilerParams(dimension_semantics=sem, vmem_limit_bytes=vmem)


def _ada_kernel(c_ref, w_ref, b_ref, o_ref):
    c = c_ref[...]
    s = c * jax.nn.sigmoid(c)
    o_ref[...] = _dot(s.astype(BF16), w_ref[...].astype(BF16)) + b_ref[...]


def _ada(c_pad, w_ada, b_ada):
    rows, d = c_pad.shape
    n = w_ada.shape[1]
    tn = n // 6
    return pl.pallas_call(
        _ada_kernel,
        out_shape=jax.ShapeDtypeStruct((rows, n), F32),
        grid=(n // tn,),
        in_specs=[pl.BlockSpec((rows, d), lambda j: (0, 0)),
                  pl.BlockSpec((d, tn), lambda j: (0, j)),
                  pl.BlockSpec((1, tn), lambda j: (0, j))],
        out_specs=pl.BlockSpec((rows, tn), lambda j: (0, j)),
        compiler_params=_params(("arbitrary",)),
        name="ada",
    )(c_pad, w_ada, b_ada.reshape(1, n))


def _rms_mod(x, g, sc, sh):
    ms = jnp.mean(x * x, axis=-1, keepdims=True)
    y = x * lax.rsqrt(ms + EPS) * g
    return y * (1.0 + sc) + sh


def _head_rms(t, bd, g):
    sq = t * t
    hi = sq.astype(BF16)
    lo = (sq - hi.astype(F32)).astype(BF16)
    ms = _dot(hi, bd) + _dot(lo, bd)
    return t * lax.rsqrt(ms + EPS) * g


def _inproj_kernel(x_ref, sh_ref, sc_ref, g_ref, w_ref, cw_ref, qg_ref, kg_ref, bd_ref, vone_ref,
                   ya_ref, q_ref, k_ref, v_ref, qi_ref, ki_ref, wi_ref, ubuf, *, tm, cw, aw):
    s = pl.program_id(1)
    hb = _rms_mod(x_ref[0], g_ref[...], sc_ref[0], sh_ref[0]).astype(BF16)

    zc = _dot(hb, w_ref[:, 0:3 * cw])
    xin, gb, gc = zc[:, 0:cw], zc[:, cw:2 * cw], zc[:, 2 * cw:3 * cw]
    u = gc * xin

    @pl.when(s == 0)
    def _():
        ubuf[0:SUBLANES, :] = jnp.zeros((SUBLANES, cw), F32)

    ubuf[SUBLANES:SUBLANES + tm, :] = u
    u1 = ubuf[SUBLANES - 1:SUBLANES - 1 + tm, :]
    u2 = ubuf[SUBLANES - 2:SUBLANES - 2 + tm, :]
    conv = cw_ref[0:1, :] * u2 + cw_ref[1:2, :] * u1 + cw_ref[2:3, :] * u
    ya_ref[0] = (gb * conv).astype(BF16)
    ubuf[0:SUBLANES, :] = ubuf[tm:tm + SUBLANES, :]

    o = 3 * cw
    bd = bd_ref[...]
    q = _dot(hb, w_ref[:, o:o + aw])
    q_ref[0] = (_head_rms(q, bd, qg_ref[...]) * (HEAD_DIM ** -0.5)).astype(BF16)
    k = _dot(hb, w_ref[:, o + aw:o + 2 * aw])
    k_ref[0] = _head_rms(k, bd, kg_ref[...]).astype(BF16)
    vw = N_HEADS * LANES
    v_ref[0] = (_dot(hb, w_ref[:, o + 2 * aw:o + 2 * aw + vw]) + vone_ref[...]).astype(BF16)

    o = o + 2 * aw + vw
    qw = IDX_HEADS * LANES
    qi_ref[0] = _dot(hb, w_ref[:, o:o + qw]).astype(BF16)
    ki_ref[0] = _dot(hb, w_ref[:, o + qw:o + qw + LANES]).astype(BF16)
    wi_ref[0] = _dot(hb, w_ref[:, o + qw + LANES:o + qw + 2 * LANES]) * ((IDX_HEADS * IDX_DIM) ** -0.5)


def _inproj(x, sh1, sc1, g1n, w_main, conv_w, qg, kg, bd, vone, *, tm):
    b, s, d = x.shape
    cw = conv_w.shape[1]
    aw = N_HEADS * HEAD_DIM
    vw = N_HEADS * LANES
    qw = IDX_HEADS * LANES
    row = lambda width: pl.BlockSpec((1, tm, width), lambda bi, si: (bi, si, 0))
    mod = pl.BlockSpec((1, 1, d), lambda bi, si: (bi, 0, 0))
    full = lambda a: pl.BlockSpec(a.shape, lambda bi, si: (0,) * a.ndim)
    outs = [jax.ShapeDtypeStruct((b, s, cw), BF16)] + [jax.ShapeDtypeStruct((b, s, aw), BF16)] * 2 + [
        jax.ShapeDtypeStruct((b, s, vw), BF16),
        jax.ShapeDtypeStruct((b, s, qw), BF16), jax.ShapeDtypeStruct((b, s, LANES), BF16),
        jax.ShapeDtypeStruct((b, s, LANES), F32)]
    return pl.pallas_call(
        functools.partial(_inproj_kernel, tm=tm, cw=cw, aw=aw),
        out_shape=outs,
        grid=(b, s // tm),
        in_specs=[row(d), mod, mod, full(g1n), full(w_main), full(conv_w), full(qg), full(kg), full(bd), full(vone)],
        out_specs=[row(cw), row(aw), row(aw), row(vw), row(qw), row(LANES), row(LANES)],
        scratch_shapes=[pltpu.VMEM((tm + SUBLANES, cw), F32)],
        compiler_params=_params(("arbitrary", "arbitrary")),
        name="inproj",
    )(x, sh1, sc1, g1n, w_main, conv_w, qg, kg, bd, vone)


def _sort_key(score):
    bits = pltpu.bitcast(score, I32)
    return jnp.where(bits < 0, INT_MIN - bits, bits)


def _attn_kernel(q_ref, k_ref, v_ref, qi_ref, ki_ref, wi_ref, ut_ref, o_ref,
                 key_scr, hi_scr, lo_scr, lom_scr, t_scr, need_scr, m_scr, acc_scr, *, qb, topk):
    i = pl.program_id(1)
    kt = qb
    ncol = kt // LANES
    npair = N_HEADS // 2
    neg_inf = -jnp.inf

    qi = qi_ref[0]
    q4 = jnp.concatenate([qi[:, h * LANES:(h + 1) * LANES] for h in range(IDX_HEADS)], axis=0)
    wi = wi_ref[0]
    wrep = [jnp.broadcast_to(wi[:, h:h + 1], (qb, kt)) for h in range(IDX_HEADS)]
    row_chunk_end = ((lax.broadcasted_iota(I32, (qb, kt), 0) + i * qb) // CHUNK + 1) * CHUNK

    def score_tile(j, carry):
        off = pl.multiple_of(j * kt, kt)
        dots = _dot_nt(q4, ki_ref[0, pl.ds(off, kt), :])
        sc = wrep[0] * jnp.maximum(dots[0:qb], 0.0)
        for h in range(1, IDX_HEADS):
            sc = sc + wrep[h] * jnp.maximum(dots[h * qb:(h + 1) * qb], 0.0)
        key = _sort_key(sc)
        kpos = lax.broadcasted_iota(I32, (qb, kt), 1) + j * kt
        key = jnp.where(kpos < row_chunk_end, key, INT_MIN)
        key_scr[:, pl.ds(off, kt)] = key
        hi_scr[:, pl.ds(off, kt)] = (key >> 16).astype(I16)
        lo_scr[:, pl.ds(off, kt)] = (key ^ 0x8000).astype(I16)
        return carry

    lax.fori_loop(0, i + 1, score_tile, 0)
    pad_off = pl.multiple_of((i + 1) * kt, kt)
    hi_scr[:, pl.ds(pad_off, kt)] = jnp.full((qb, kt), I16_MIN, I16)
    lo_scr[:, pl.ds(pad_off, kt)] = jnp.full((qb, kt), I16_MIN, I16)

    one16 = jnp.ones((qb, LANES), I16)
    zero16 = jnp.zeros((qb, LANES), I16)

    def row_count(term):
        def body(jj, acc):
            off = pl.multiple_of(jj * (2 * kt), 2 * kt)
            for c in range(2 * ncol):
                acc = acc + term(off + c * LANES)
            return acc
        acc = lax.fori_loop(0, (i + 2) // 2, body, zero16)
        tot = jnp.sum(acc.astype(I32).astype(F32), axis=1, keepdims=True)
        return jnp.broadcast_to(tot, (qb, LANES))

    def search16(count_ge, target):
        zero = jnp.zeros((qb, LANES), I32)
        start = jnp.where(count_ge(zero16) >= target, zero, jnp.full((qb, LANES), I16_MIN, I32))

        def bit_step(it, res):
            cand = res + lax.shift_left(jnp.int32(1), jnp.int32(14) - it)
            return jnp.where(count_ge(cand.astype(I16)) >= target, cand, res)

        return lax.fori_loop(0, 15, bit_step, start)

    @pl.when(i == 0)
    def _():
        t_scr[...] = jnp.full((qb, LANES), INT_MIN + 1, I32)
        need_scr[...] = jnp.full((qb, LANES), float(topk), F32)

    @pl.when(i > 0)
    def _():
        kf = jnp.full((qb, LANES), float(topk), F32)
        hi_col = lambda o: hi_scr[:, pl.ds(o, LANES)]
        lo_col = lambda o: lo_scr[:, pl.ds(o, LANES)]
        t_hi = search16(lambda c: row_count(lambda o: jnp.where(hi_col(o) >= c, one16, zero16)), kf)
        t_hi16 = t_hi.astype(I16)
        above = row_count(lambda o: jnp.where(hi_col(o) > t_hi16, one16, zero16))

        min16 = jnp.full((qb, LANES), I16_MIN, I16)

        def mask_low(jj, carry):
            off = pl.multiple_of(jj * (2 * kt), 2 * kt)
            for c in range(2 * ncol):
                o = off + c * LANES
                lom_scr[:, pl.ds(o, LANES)] = jnp.where(hi_col(o) == t_hi16, lo_col(o), min16)
            return carry

        lax.fori_loop(0, (i + 2) // 2, mask_low, 0)
        lom_col = lambda o: lom_scr[:, pl.ds(o, LANES)]
        want = kf - above
        t_lo = search16(lambda c: row_count(lambda o: jnp.where(lom_col(o) >= c, one16, zero16)), want)
        t_lo16 = t_lo.astype(I16)
        beyond = row_count(lambda o: jnp.where(lom_col(o) > t_lo16, one16, zero16))
        t_scr[...] = t_hi * 65536 + (t_lo + 32768)
        need_scr[...] = want - beyond

    m_scr[...] = jnp.full(m_scr.shape, jnp.finfo(F32).min, F32)
    acc_scr[...] = jnp.zeros(acc_scr.shape, F32)
    lane = lax.broadcasted_iota(I32, (qb, LANES), 1)
    low_half = lane < HEAD_DIM
    q = q_ref[0]
    zero_q = jnp.zeros((qb, LANES), BF16)
    qm = []
    for h in range(N_HEADS):
        qp = q[:, (h // 2) * LANES:(h // 2 + 1) * LANES]
        qm.append(jnp.where(low_half if h % 2 == 0 else jnp.logical_not(low_half), qp, zero_q))
    thr = jnp.concatenate([t_scr[...]] * ncol, axis=1)
    need = jnp.concatenate([need_scr[...]] * ncol, axis=1)
    ut = ut_ref[...]

    def attn_tile(j, run):
        off = pl.multiple_of(j * kt, kt)
        key = key_scr[:, pl.ds(off, kt)]
        eq = key == thr
        cnt = _dot(jnp.where(eq, 1.0, 0.0).astype(BF16), ut) + run
        sel = jnp.logical_or(key > thr, jnp.logical_and(eq, cnt <= need))
        bias = jnp.where(sel, 0.0, neg_inf)
        for h in range(N_HEADS):
            p2 = h // 2
            kp = k_ref[0, pl.ds(off, kt), p2 * LANES:(p2 + 1) * LANES]
            vh = v_ref[0, pl.ds(off, kt), h * LANES:(h + 1) * LANES]
            s = _dot_nt(qm[h], kp) + bias
            m_old = m_scr[h]
            m_new = jnp.maximum(m_old, jnp.broadcast_to(jnp.max(s, axis=1, keepdims=True), (qb, LANES)))
            alpha = jnp.exp(m_old - m_new)
            p = jnp.exp(s - jnp.concatenate([m_new] * ncol, axis=1))
            acc_scr[h] = alpha * acc_scr[h] + _dot(p.astype(BF16), vh)
            m_scr[h] = m_new
        return jnp.broadcast_to(cnt[:, kt - 1:kt], (qb, kt))

    lax.fori_loop(0, i + 1, attn_tile, jnp.zeros((qb, kt), F32))

    def head_out(h):
        a = acc_scr[h]
        return a / jnp.broadcast_to(a[:, HEAD_DIM:HEAD_DIM + 1], (qb, LANES))

    for p2 in range(npair):
        hi = pltpu.roll(head_out(2 * p2 + 1), HEAD_DIM, 1)
        o_ref[0, :, p2 * LANES:(p2 + 1) * LANES] = jnp.where(low_half, head_out(2 * p2), hi).astype(BF16)


def _attn(q, k, v, qi, ki, wi, ut, *, qb):
    b, s, aw = q.shape
    topk = min(IDX_TOPK, s // 4)
    assert topk == qb and s % qb == 0 and qb % CHUNK == 0
    rowq = lambda width: pl.BlockSpec((1, qb, width), lambda bi, i: (bi, i, 0))
    allk = lambda width: pl.BlockSpec((1, s, width), lambda bi, i: (bi, 0, 0))
    return pl.pallas_call(
        functools.partial(_attn_kernel, qb=qb, topk=topk),
        out_shape=jax.ShapeDtypeStruct((b, s, aw), BF16),
        grid=(b, s // qb),
        in_specs=[rowq(aw), allk(aw), allk(v.shape[2]), rowq(qi.shape[2]), allk(LANES), rowq(LANES),
                  pl.BlockSpec(ut.shape, lambda bi, i: (0, 0))],
        out_specs=rowq(aw),
        scratch_shapes=[pltpu.VMEM((qb, s), I32), pltpu.VMEM((qb, s + qb), I16), pltpu.VMEM((qb, s + qb), I16),
                        pltpu.VMEM((qb, s + qb), I16), pltpu.VMEM((qb, LANES), I32), pltpu.VMEM((qb, LANES), F32),
                        pltpu.VMEM((N_HEADS, qb, LANES), F32), pltpu.VMEM((N_HEADS, qb, LANES), F32)],
        compiler_params=_params(("arbitrary", "arbitrary")),
        name="attn",
    )(q, k, v, qi, ki, wi, ut)


def _rows_to_tiles(tile_ref, x, rows):
    for sl in range(x.shape[1] // LANES):
        tile_ref[pl.ds(sl, rows, stride=SUBLANES), :] = x[:, sl * LANES:(sl + 1) * LANES]


def _tiles_to_rows(tile_ref, rows, lead=()):
    return jnp.concatenate(
        [tile_ref[lead + (pl.ds(sl, rows, stride=SUBLANES), slice(None))] for sl in range(SUBLANES)], axis=1)


def _tile_rows(t):
    return pl.ds(pl.multiple_of(t * SUBLANES, SUBLANES), SUBLANES)


def _merge_kernel(x_ref, ya_ref, yb_ref, sh1_ref, sc1_ref, g1_ref, sh2_ref, sc2_ref, n1_ref, n2_ref,
                  wg_ref, pa_ref, pb_ref, wo_ref, wr_ref, br_ref, tri_ref,
                  x1_ref, h2_ref, meta_ref, cnt_ref, carry, *, tm, d):
    first = jnp.logical_and(pl.program_id(0) == 0, pl.program_id(1) == 0)

    @pl.when(first)
    def _():
        carry[...] = jnp.zeros(carry.shape, F32)

    x = x_ref[0]
    hb = _rms_mod(x, n1_ref[...], sc1_ref[0], sh1_ref[0]).astype(BF16)
    ga = jax.nn.sigmoid(_dot(hb, wg_ref[:, 0:d]))
    gbm = jax.nn.sigmoid(_dot(hb, wg_ref[:, d:2 * d]))
    merged = ga * _dot(ya_ref[0], pa_ref[...]) + gbm * _dot(yb_ref[0], pb_ref[...])
    x1 = x + g1_ref[0] * _dot(merged.astype(BF16), wo_ref[...])
    x1_ref[0] = x1
    h2 = _rms_mod(x1, n2_ref[...], sc2_ref[0], sh2_ref[0])
    _rows_to_tiles(h2_ref, h2, tm)

    lane = lax.broadcasted_iota(I32, (tm, LANES), 1).astype(F32)
    logits = _dot(h2.astype(BF16), wr_ref[...]) + br_ref[...]
    logits = jnp.where(lane < N_EXPERTS, logits, -jnp.inf)
    vals, idxs = [], []
    hot = jnp.zeros((tm, LANES), F32)
    for _ in range(TOP_K):
        mx = jnp.max(logits, axis=1, keepdims=True)
        ix = jnp.min(jnp.where(logits == mx, lane, float(LANES)), axis=1, keepdims=True)
        pick = lane == ix
        hot = jnp.where(pick, 1.0, hot)
        logits = jnp.where(pick, -jnp.inf, logits)
        vals.append(mx)
        idxs.append(ix)
    es = [jnp.exp(vv - vals[0]) for vv in vals]
    den = es[0] + es[1] + es[2] + es[3]

    prefix = _dot(tri_ref[...], hot.astype(BF16)) + carry[0:1, :]
    carry[...] = jnp.broadcast_to(prefix[tm - 1:tm, :] + hot[tm - 1:tm, :], carry.shape)
    cnt_ref[...] = carry[...]
    meta = jnp.zeros((tm, LANES), F32)
    for kk in range(TOP_K):
        rank = jnp.sum(jnp.where(lane == idxs[kk], prefix, 0.0), axis=1, keepdims=True)
        meta = jnp.where(lane == float(kk), idxs[kk], meta)
        meta = jnp.where(lane == float(TOP_K + kk), rank, meta)
        meta = jnp.where(lane == float(2 * TOP_K + kk), es[kk] / den, meta)
    meta_ref[...] = meta


def _merge(x, ya, yb, mods, n1, n2, w_gate, pa, pb, wo, wr, br, tri, *, tm):
    b, s, d = x.shape
    t = b * s
    sh1, sc1, g1, sh2, sc2 = mods
    nst = s // tm
    row = lambda width: pl.BlockSpec((1, tm, width), lambda bi, si: (bi, si, 0))
    mod = pl.BlockSpec((1, 1, d), lambda bi, si: (bi, 0, 0))
    full = lambda a: pl.BlockSpec(a.shape, lambda bi, si: (0,) * a.ndim)
    outs = [jax.ShapeDtypeStruct((b, s, d), F32),
            jax.ShapeDtypeStruct((t * (d // LANES), LANES), F32),
            jax.ShapeDtypeStruct((t, LANES), F32),
            jax.ShapeDtypeStruct((SUBLANES, LANES), F32)]
    return pl.pallas_call(
        functools.partial(_merge_kernel, tm=tm, d=d),
        out_shape=outs,
        grid=(b, nst),
        in_specs=[row(d), row(ya.shape[2]), row(yb.shape[2]), mod, mod, mod, mod, mod, full(n1), full(n2),
                  full(w_gate), full(pa), full(pb), full(wo), full(wr), full(br), full(tri)],
        out_specs=[row(d),
                   pl.BlockSpec((tm * (d // LANES), LANES), lambda bi, si: (bi * nst + si, 0)),
                   pl.BlockSpec((tm, LANES), lambda bi, si: (bi * nst + si, 0)),
                   pl.BlockSpec((SUBLANES, LANES), lambda bi, si: (0, 0))],
        scratch_shapes=[pltpu.VMEM((SUBLANES, LANES), F32)],
        compiler_params=_params(("arbitrary", "arbitrary")),
        name="merge",
    )(x, ya, yb, sh1, sc1, g1, sh2, sc2, n1, n2, w_gate, pa, pb, wo, wr, br, tri)


def _plan_kernel(meta_ref, cnt_ref, tri_ref, dest_ref, be_ref, *, tm, nbp):
    lane_i = lax.broadcasted_iota(I32, (tm, LANES), 1)
    lane = lane_i.astype(F32)
    nblk = jnp.floor((cnt_ref[...] + float(EXPERT_BLOCK - 1)) * (1.0 / EXPERT_BLOCK))
    pend = _dot(nblk.astype(BF16), tri_ref[...])
    pstart = ((pend - nblk) * float(EXPERT_BLOCK))[0:1, :]
    meta = meta_ref[...]
    dest = jnp.zeros((tm, LANES), F32)
    for kk in range(TOP_K):
        ix = meta[:, kk:kk + 1]
        rank = meta[:, TOP_K + kk:TOP_K + kk + 1]
        base = jnp.sum(jnp.where(lane == ix, pstart, 0.0), axis=1, keepdims=True)
        dest = jnp.where(lane == float(kk), base + rank, dest)
    dest_ref[...] = dest.astype(I32)

    blk = lax.broadcasted_iota(I32, (nbp, LANES), 0).astype(F32)
    lane_b = lax.broadcasted_iota(I32, (nbp, LANES), 1)
    ends = jnp.where(lane_b < N_EXPERTS, pend[0:1, :], float(2 ** 30))
    e_of = jnp.sum(jnp.where(ends <= blk, 1.0, 0.0), axis=1, keepdims=True)
    e_of = jnp.minimum(e_of, float(N_EXPERTS - 1))
    used = jnp.sum(jnp.where(lane_b == N_EXPERTS - 1, pend[0:1, :], 0.0), axis=1, keepdims=True)
    be = jnp.where(lane_b == 0, jnp.broadcast_to(e_of, (nbp, LANES)), jnp.broadcast_to(used, (nbp, LANES)))
    be_ref[...] = be.astype(I32)


def _plan(meta, counts, tri_e, *, tm, nbp):
    t = meta.shape[0]
    return pl.pallas_call(
        functools.partial(_plan_kernel, tm=tm, nbp=nbp),
        out_shape=[jax.ShapeDtypeStruct((t, LANES), I32), jax.ShapeDtypeStruct((nbp, LANES), I32)],
        grid=(t // tm,),
        in_specs=[pl.BlockSpec((tm, LANES), lambda i: (i, 0)),
                  pl.BlockSpec((SUBLANES, LANES), lambda i: (0, 0)),
                  pl.BlockSpec((LANES, LANES), lambda i: (0, 0))],
        out_specs=[pl.BlockSpec((tm, LANES), lambda i: (i, 0)),
                   pl.BlockSpec((nbp, LANES), lambda i: (0, 0))],
        compiler_params=_params(("arbitrary",)),
        name="plan",
    )(meta, counts, tri_e)


def _dispatch_kernel(dest_ref, h_ref, xs_in_ref, xs_ref, sem, *, tb):
    del xs_in_ref

    def row_copy(t, kk):
        return pltpu.make_async_copy(h_ref.at[_tile_rows(t)], xs_ref.at[_tile_rows(dest_ref[t * TOP_K + kk])], sem)

    def issue(t, c):
        for kk in range(TOP_K):
            row_copy(t, kk).start()
        return c

    lax.fori_loop(0, tb, issue, 0, unroll=4)

    def drain(t, c):
        for kk in range(TOP_K):
            row_copy(t, kk).wait()
        return c

    lax.fori_loop(0, tb, drain, 0, unroll=8)


def _dispatch(dest_flat, h2, xs_zero, *, tb):
    t = h2.shape[0] // SUBLANES
    return pl.pallas_call(
        functools.partial(_dispatch_kernel, tb=tb),
        out_shape=jax.ShapeDtypeStruct(xs_zero.shape, xs_zero.dtype),
        grid=(t // tb,),
        in_specs=[pl.BlockSpec((tb * TOP_K,), lambda i: (i,), memory_space=pltpu.SMEM),
                  pl.BlockSpec((tb * SUBLANES, LANES), lambda i: (i, 0)),
                  pl.BlockSpec(memory_space=pl.ANY)],
        out_specs=pl.BlockSpec(memory_space=pl.ANY),
        scratch_shapes=[pltpu.SemaphoreType.DMA(())],
        input_output_aliases={2: 0},
        compiler_params=_params(("arbitrary",)),
        name="dispatch",
    )(dest_flat, h2, xs_zero)


def _ffn_kernel(be_ref, used_ref, xs_ref, wgu_ref, bgu_ref, wd_ref, bd_ref, ys_ref, wgu_bf, wd_bf, last_e,
                *, d, dff):
    i = pl.program_id(0)
    e = be_ref[i]

    @pl.when(i == 0)
    def _():
        last_e[0] = -1

    @pl.when(i < used_ref[0])
    def _():
        @pl.when(last_e[0] != e)
        def _():
            wgu_bf[...] = wgu_ref[0].astype(BF16)
            wd_bf[...] = wd_ref[0].astype(BF16)
            last_e[0] = e

        xb = _tiles_to_rows(xs_ref, EXPERT_BLOCK).astype(BF16)
        gu = _dot(xb, wgu_bf[...]) + bgu_ref[0]
        x_glu = jnp.minimum(gu[:, 0:dff], SWIGLU_LIMIT)
        x_lin = jnp.clip(gu[:, dff:2 * dff], -SWIGLU_LIMIT, SWIGLU_LIMIT)
        act = x_glu * jax.nn.sigmoid(SWIGLU_ALPHA * x_glu) * (x_lin + 1.0)
        y = _dot(act.astype(BF16), wd_bf[...]) + bd_ref[0]
        _rows_to_tiles(ys_ref, y, EXPERT_BLOCK)

    @pl.when(i >= used_ref[0])
    def _():
        ys_ref[...] = jnp.zeros(ys_ref.shape, F32)


def _ffn(block_e, used, xs, w_gate_up, b_gate_up, w_down, b_down):
    ne, d, dff2 = w_gate_up.shape
    dff = dff2 // 2
    blk = EXPERT_BLOCK * SUBLANES
    nblocks = xs.shape[0] // blk
    grid_spec = pltpu.PrefetchScalarGridSpec(
        num_scalar_prefetch=2,
        grid=(nblocks,),
        in_specs=[pl.BlockSpec((blk, LANES), lambda i, be, us: (i, 0)),
                  pl.BlockSpec((1, d, dff2), lambda i, be, us: (be[i], 0, 0)),
                  pl.BlockSpec((1, 1, dff2), lambda i, be, us: (be[i], 0, 0)),
                  pl.BlockSpec((1, dff, d), lambda i, be, us: (be[i], 0, 0)),
                  pl.BlockSpec((1, 1, d), lambda i, be, us: (be[i], 0, 0))],
        out_specs=pl.BlockSpec((blk, LANES), lambda i, be, us: (i, 0)),
        scratch_shapes=[pltpu.VMEM((d, dff2), BF16), pltpu.VMEM((dff, d), BF16), pltpu.SMEM((1,), I32)])
    return pl.pallas_call(
        functools.partial(_ffn_kernel, d=d, dff=dff),
        out_shape=jax.ShapeDtypeStruct(xs.shape, F32),
        grid_spec=grid_spec,
        compiler_params=_params(("arbitrary",)),
        name="ffn",
    )(block_e, used, xs, w_gate_up, b_gate_up.reshape(ne, 1, dff2), w_down, b_down.reshape(ne, 1, d))


def _combine_kernel(dest_ref, dest_next_ref, ys_ref, x1_ref, g2_ref, meta_ref, o_ref, buf, sems, *, tb, nsteps):
    step = pl.program_id(0) * pl.num_programs(1) + pl.program_id(1)
    slot = step % 2

    def row_copy(idx_ref, sl, t, kk):
        return pltpu.make_async_copy(ys_ref.at[_tile_rows(idx_ref[t * TOP_K + kk])],
                                     buf.at[sl, kk, _tile_rows(t)], sems.at[sl])

    def gather(idx_ref, sl):
        def issue(t, c):
            for kk in range(TOP_K):
                row_copy(idx_ref, sl, t, kk).start()
            return c
        lax.fori_loop(0, tb, issue, 0, unroll=4)

    @pl.when(step == 0)
    def _():
        gather(dest_ref, 0)

    @pl.when(step + 1 < nsteps)
    def _():
        gather(dest_next_ref, 1 - slot)

    def drain(t, c):
        for kk in range(TOP_K):
            row_copy(dest_ref, slot, t, kk).wait()
        return c

    lax.fori_loop(0, tb, drain, 0, unroll=8)

    meta = meta_ref[...]
    moe = None
    for kk in range(TOP_K):
        yk = _tiles_to_rows(buf, tb, lead=(slot, kk))
        term = yk * meta[:, 2 * TOP_K + kk:2 * TOP_K + kk + 1]
        moe = term if moe is None else moe + term
    o_ref[0] = x1_ref[0] + g2_ref[0] * moe


def _combine(dest_flat, ys, x1, g2, meta, *, tb):
    b, s, d = x1.shape
    nst = s // tb
    nsteps = b * nst
    return pl.pallas_call(
        functools.partial(_combine_kernel, tb=tb, nsteps=nsteps),
        out_shape=jax.ShapeDtypeStruct((b, s, d), F32),
        grid=(b, nst),
        in_specs=[pl.BlockSpec((tb * TOP_K,), lambda bi, si: (bi * nst + si,), memory_space=pltpu.SMEM),
                  pl.BlockSpec((tb * TOP_K,), lambda bi, si: (jnp.minimum(bi * nst + si + 1, nsteps - 1),),
                               memory_space=pltpu.SMEM),
                  pl.BlockSpec(memory_space=pl.ANY),
                  pl.BlockSpec((1, tb, d), lambda bi, si: (bi, si, 0)),
                  pl.BlockSpec((1, 1, d), lambda bi, si: (bi, 0, 0)),
                  pl.BlockSpec((tb, LANES), lambda bi, si: (bi * nst + si, 0))],
        out_specs=pl.BlockSpec((1, tb, d), lambda bi, si: (bi, si, 0)),
        scratch_shapes=[pltpu.VMEM((2, TOP_K, tb * SUBLANES, LANES), F32), pltpu.SemaphoreType.DMA((2,))],
        compiler_params=_params(("arbitrary", "arbitrary")),
        name="combine",
    )(dest_flat, dest_flat, ys, x1, g2, meta)


def _tri(n, strict, upper):
    r = lax.broadcasted_iota(I32, (n, n), 0)
    c = lax.broadcasted_iota(I32, (n, n), 1)
    if upper:
        m = (r < c) if strict else (r <= c)
    else:
        m = (c < r) if strict else (c <= r)
    return m.astype(BF16)


def _layer(x, c_pad, w_ada, b_ada, norm1_g, w_in, conv_w, q_norm_g, k_norm_g, w_conv_out, w_attn_out, w_o,
           norm2_g, w_router, b_router, w_gate_up, b_gate_up, w_down, b_down):
    b, s, d = x.shape
    assert d == SUBLANES * LANES
    t = b * s
    cw = conv_w.shape[1]
    aw = N_HEADS * HEAD_DIM
    tm = min(512, s)
    qb = min(IDX_TOPK, s // 4)
    tb = 256

    mod = _ada(c_pad, w_ada, b_ada)[:b]
    sh1, sc1, g1, sh2, sc2, g2 = [m.reshape(b, 1, d) for m in jnp.split(mod, 6, axis=-1)]

    o = 3 * cw + 3 * aw
    w_qi = w_in[:, o:o + IDX_HEADS * IDX_DIM].reshape(d, IDX_HEADS, IDX_DIM)
    w_qi = jnp.pad(w_qi, ((0, 0), (0, 0), (0, LANES - IDX_DIM))).reshape(d, IDX_HEADS * LANES)
    o2 = o + IDX_HEADS * IDX_DIM
    w_ki = jnp.pad(w_in[:, o2:o2 + IDX_DIM], ((0, 0), (0, LANES - IDX_DIM)))
    o3 = o2 + IDX_DIM
    w_wi = jnp.pad(w_in[:, o3:o3 + IDX_HEADS], ((0, 0), (0, LANES - IDX_HEADS)))
    o4 = o3 + IDX_HEADS
    ov = 3 * cw + 2 * aw
    w_v = jnp.pad(w_in[:, ov:ov + aw].reshape(d, N_HEADS, HEAD_DIM), ((0, 0), (0, 0), (0, LANES - HEAD_DIM)))
    vone = (lax.broadcasted_iota(I32, (1, N_HEADS * LANES), 1) % LANES == HEAD_DIM).astype(F32)
    w_main = jnp.concatenate([w_in[:, :ov], w_v.reshape(d, N_HEADS * LANES), w_qi, w_ki, w_wi], axis=1).astype(BF16)
    w_gate = w_in[:, o4:o4 + 2 * d].astype(BF16)
    seg = lax.broadcasted_iota(I32, (aw, aw), 0) // HEAD_DIM == lax.broadcasted_iota(I32, (aw, aw), 1) // HEAD_DIM
    bd = jnp.where(seg, 1.0 / HEAD_DIM, 0.0).astype(BF16)
    qg = jnp.tile(q_norm_g, N_HEADS).reshape(1, aw)
    kg = jnp.tile(k_norm_g, N_HEADS).reshape(1, aw)

    ya, q, k, v, qi, ki, wi = _inproj(x, sh1, sc1, norm1_g.reshape(1, d), w_main, conv_w, qg, kg, bd, vone, tm=tm)
    yb = _attn(q, k, v, qi, ki, wi, _tri(qb, strict=False, upper=True), qb=qb)

    wr = jnp.pad(w_router, ((0, 0), (0, LANES - N_EXPERTS))).astype(BF16)
    br = jnp.pad(b_router, (0, LANES - N_EXPERTS)).reshape(1, LANES)
    x1, h2, meta, counts = _merge(
        x, ya, yb, (sh1, sc1, g1, sh2, sc2), norm1_g.reshape(1, d), norm2_g.reshape(1, d), w_gate,
        w_conv_out.astype(BF16), w_attn_out.astype(BF16), w_o.astype(BF16), wr, br,
        _tri(tm, strict=True, upper=False), tm=tm)

    nk = t * TOP_K
    cap = -(-nk // EXPERT_BLOCK) * EXPERT_BLOCK + N_EXPERTS * EXPERT_BLOCK
    nblocks = cap // EXPERT_BLOCK
    nbp = -(-nblocks // SUBLANES) * SUBLANES
    dest_pad, be_pad = _plan(meta, counts, _tri(LANES, strict=False, upper=True), tm=tm, nbp=nbp)
    dest_flat = dest_pad[:, :TOP_K].reshape(-1)
    block_e = be_pad[:nblocks, 0]
    used = be_pad[0:1, 1]

    xs = _dispatch(dest_flat, h2, jnp.zeros((cap * SUBLANES, LANES), F32), tb=tb)
    ys = _ffn(block_e, used, xs, w_gate_up, b_gate_up, w_down, b_down)
    return _combine(dest_flat, ys, x1, g2, meta, tb=tb)


def kernel(x, c, w_ada, b_ada, norm1_g, w_in, conv_w, q_norm_g, k_norm_g, w_conv_out, w_attn_out, w_o, norm2_g,
           w_router, b_router, w_gate_up, b_gate_up, w_down, b_down):
    c_pad = jnp.pad(c, ((0, -c.shape[0] % SUBLANES), (0, 0)))
    for l in range(w_ada.shape[0]):
        x = _layer(x, c_pad, w_ada[l], b_ada[l], norm1_g[l], w_in[l], conv_w[l], q_norm_g[l], k_norm_g[l],
                   w_conv_out[l], w_attn_out[l], w_o[l], norm2_g[l], w_router[l], b_router[l],
                   w_gate_up[l], b_gate_up[l], w_down[l], b_down[l])
    return x
```

```python
import functools

import jax
import jax.numpy as jnp
from jax import lax
from jax.experimental import pallas as pl
from jax.experimental.pallas import tpu as pltpu

F32 = jnp.float32
BF16 = jnp.bfloat16
I32 = jnp.int32

LANES = 128
SUBLANES = 8

CHUNK = 64
CONV_K = 3
N_HEADS = 8
HEAD_DIM = 64
IDX_HEADS = 4
IDX_DIM = 64
IDX_TOPK = 256
N_EXPERTS = 32
TOP_K = 4
SWIGLU_LIMIT = 7.0
SWIGLU_ALPHA = 1.702
EXPERT_BLOCK = 256
EPS = 1e-6

INT_MIN = -(2 ** 31)
VMEM_LIMIT = 56 * 1024 * 1024


def _dot(a, b):
    return jnp.dot(a, b, preferred_element_type=F32)


def _dot_nt(a, b):
    return lax.dot_general(a, b, (((1,), (1,)), ((), ())), preferred_element_type=F32)


def _params(sem, vmem=VMEM_LIMIT):
    return pltpu.CompilerParams(dimension_semantics=sem, vmem_limit_bytes=vmem)


def _ada_kernel(c_ref, w_ref, b_ref, o_ref):
    c = c_ref[...]
    s = c * jax.nn.sigmoid(c)
    o_ref[...] = _dot(s.astype(BF16), w_ref[...].astype(BF16)) + b_ref[...]


def _ada(c_pad, w_ada, b_ada):
    rows, d = c_pad.shape
    n = w_ada.shape[1]
    tn = n // 6
    return pl.pallas_call(
        _ada_kernel,
        out_shape=jax.ShapeDtypeStruct((rows, n), F32),
        grid=(n // tn,),
        in_specs=[pl.BlockSpec((rows, d), lambda j: (0, 0)),
                  pl.BlockSpec((d, tn), lambda j: (0, j)),
                  pl.BlockSpec((1, tn), lambda j: (0, j))],
        out_specs=pl.BlockSpec((rows, tn), lambda j: (0, j)),
        compiler_params=_params(("arbitrary",)),
        name="ada",
    )(c_pad, w_ada, b_ada.reshape(1, n))


def _rms_mod(x, g, sc, sh):
    ms = jnp.mean(x * x, axis=-1, keepdims=True)
    y = x * lax.rsqrt(ms + EPS) * g
    return y * (1.0 + sc) + sh


def _head_rms(t, bd, g):
    sq = t * t
    hi = sq.astype(BF16)
    lo = (sq - hi.astype(F32)).astype(BF16)
    ms = _dot(hi, bd) + _dot(lo, bd)
    return t * lax.rsqrt(ms + EPS) * g


def _inproj_kernel(x_ref, sh_ref, sc_ref, g_ref, w_ref, cw_ref, qg_ref, kg_ref, bd_ref, vone_ref,
                   ya_ref, q_ref, k_ref, v_ref, qi_ref, ki_ref, wit_ref, ubuf, *, tm, cw, aw):
    s = pl.program_id(1)
    hb = _rms_mod(x_ref[0], g_ref[...], sc_ref[0], sh_ref[0]).astype(BF16)

    zc = _dot(hb, w_ref[:, 0:3 * cw])
    xin, gb, gc = zc[:, 0:cw], zc[:, cw:2 * cw], zc[:, 2 * cw:3 * cw]
    u = gc * xin

    @pl.when(s == 0)
    def _():
        ubuf[0:SUBLANES, :] = jnp.zeros((SUBLANES, cw), F32)

    ubuf[SUBLANES:SUBLANES + tm, :] = u
    u1 = ubuf[SUBLANES - 1:SUBLANES - 1 + tm, :]
    u2 = ubuf[SUBLANES - 2:SUBLANES - 2 + tm, :]
    conv = cw_ref[0:1, :] * u2 + cw_ref[1:2, :] * u1 + cw_ref[2:3, :] * u
    ya_ref[0] = (gb * conv).astype(BF16)
    ubuf[0:SUBLANES, :] = ubuf[tm:tm + SUBLANES, :]

    o = 3 * cw
    bd = bd_ref[...]
    q = _dot(hb, w_ref[:, o:o + aw])
    q_ref[0] = (_head_rms(q, bd, qg_ref[...]) * (HEAD_DIM ** -0.5)).astype(BF16)
    k = _dot(hb, w_ref[:, o + aw:o + 2 * aw])
    k_ref[0] = _head_rms(k, bd, kg_ref[...]).astype(BF16)
    vw = N_HEADS * LANES
    v_ref[0] = (_dot(hb, w_ref[:, o + 2 * aw:o + 2 * aw + vw]) + vone_ref[...]).astype(BF16)

    o = o + 2 * aw + vw
    qw = IDX_HEADS * LANES
    qi_ref[0] = _dot(hb, w_ref[:, o:o + qw]).astype(BF16)
    ki_ref[0] = _dot(hb, w_ref[:, o + qw:o + qw + LANES]).astype(BF16)
    wi = _dot(hb, w_ref[:, o + qw + LANES:o + qw + 2 * LANES]) * ((IDX_HEADS * IDX_DIM) ** -0.5)
    wit_ref[0] = wi.T[0:SUBLANES, :]


def _inproj(x, sh1, sc1, g1n, w_main, conv_w, qg, kg, bd, vone, *, tm):
    b, s, d = x.shape
    cw = conv_w.shape[1]
    aw = N_HEADS * HEAD_DIM
    vw = N_HEADS * LANES
    qw = IDX_HEADS * LANES
    row = lambda width: pl.BlockSpec((1, tm, width), lambda bi, si: (bi, si, 0))
    mod = pl.BlockSpec((1, 1, d), lambda bi, si: (bi, 0, 0))
    full = lambda a: pl.BlockSpec(a.shape, lambda bi, si: (0,) * a.ndim)
    outs = [jax.ShapeDtypeStruct((b, s, cw), BF16)] + [jax.ShapeDtypeStruct((b, s, aw), BF16)] * 2 + [
        jax.ShapeDtypeStruct((b, s, vw), BF16),
        jax.ShapeDtypeStruct((b, s, qw), BF16), jax.ShapeDtypeStruct((b, s, LANES), BF16),
        jax.ShapeDtypeStruct((b, SUBLANES, s), F32)]
    return pl.pallas_call(
        functools.partial(_inproj_kernel, tm=tm, cw=cw, aw=aw),
        out_shape=outs,
        grid=(b, s // tm),
        in_specs=[row(d), mod, mod, full(g1n), full(w_main), full(conv_w), full(qg), full(kg), full(bd), full(vone)],
        out_specs=[row(cw), row(aw), row(aw), row(vw), row(qw), row(LANES),
                   pl.BlockSpec((1, SUBLANES, tm), lambda bi, si: (bi, 0, si))],
        scratch_shapes=[pltpu.VMEM((tm + SUBLANES, cw), F32)],
        compiler_params=_params(("arbitrary", "arbitrary")),
        name="inproj",
    )(x, sh1, sc1, g1n, w_main, conv_w, qg, kg, bd, vone)


def _sort_key(score):
    bits = pltpu.bitcast(score, I32)
    return jnp.where(bits < 0, INT_MIN - bits, bits)


def _attn_kernel(q_ref, k_ref, v_ref, qi_ref, ki_ref, wit_ref, ut_ref, o_ref,
                 key_scr, keyq_scr, t_scr, need_scr, m_scr, acc_scr, *, qb, topk):
    i = pl.program_id(1)
    kt = qb
    ncol = kt // LANES
    nhalf = qb // LANES
    npair = N_HEADS // 2
    neg_inf = -jnp.inf
    slab = 4 * SUBLANES

    qi = qi_ref[0]
    q4 = jnp.concatenate([qi[:, h * LANES:(h + 1) * LANES] for h in range(IDX_HEADS)], axis=0)
    wit = wit_ref[0]
    chunk_end = ((lax.broadcasted_iota(I32, (1, qb), 1) + i * qb) // CHUNK + 1) * CHUNK

    def score_tile(j, carry):
        off = pl.multiple_of(j * kt, kt)
        dots = _dot_nt(ki_ref[0, pl.ds(off, kt), :], q4)
        sc = wit[0:1, :] * jnp.maximum(dots[:, 0:qb], 0.0)
        for h in range(1, IDX_HEADS):
            sc = sc + wit[h:h + 1, :] * jnp.maximum(dots[:, h * qb:(h + 1) * qb], 0.0)
        kpos = lax.broadcasted_iota(I32, (kt, qb), 0) + j * kt
        key = jnp.where(kpos < chunk_end, _sort_key(sc), INT_MIN)
        key_scr[pl.ds(off, kt), :] = key
        keyq_scr[:, pl.ds(off, kt)] = pltpu.bitcast(pltpu.bitcast(key, F32).T, I32)
        return carry

    lax.fori_loop(0, i + 1, score_tile, 0)
    key_scr[pl.ds(pl.multiple_of((i + 1) * kt, kt), kt), :] = jnp.full((kt, qb), INT_MIN, I32)

    def count_ge(cands):
        wide = [jnp.broadcast_to(c, (slab, LANES)) for c in cands]

        def body(jj, accs):
            off = pl.multiple_of(jj * (2 * kt), 2 * kt)
            accs = list(accs)
            for r in range(2 * kt // slab):
                for g in range(nhalf):
                    kk = key_scr[pl.ds(off + r * slab, slab), g * LANES:(g + 1) * LANES]
                    accs[g] = accs[g] + jnp.where(kk >= wide[g], 1.0, 0.0)
            return tuple(accs)

        accs = lax.fori_loop(0, (i + 2) // 2, body, tuple(jnp.zeros((slab, LANES), F32) for _ in range(nhalf)))
        return [jnp.sum(a, axis=0, keepdims=True) for a in accs]

    def put_rows(ref, vals):
        for g in range(nhalf):
            ref[:, g * LANES:(g + 1) * LANES] = jnp.broadcast_to(vals[g], (LANES, LANES))

    @pl.when(i == 0)
    def _():
        t_scr[...] = jnp.full(t_scr.shape, INT_MIN + 1, I32)
        need_scr[...] = jnp.full(need_scr.shape, float(topk), F32)

    @pl.when(i > 0)
    def _():
        kf = float(topk)
        zero = jnp.zeros((1, LANES), I32)
        start = tuple(jnp.where(c >= kf, zero, jnp.full((1, LANES), INT_MIN, I32)) for c in count_ge([zero] * nhalf))

        def bit_step(it, res):
            inc = lax.shift_left(jnp.int32(1), jnp.int32(30) - it)
            cands = [r + inc for r in res]
            return tuple(jnp.where(c >= kf, ca, r) for c, ca, r in zip(count_ge(cands), cands, res))

        res = lax.fori_loop(0, 31, bit_step, start)
        put_rows(t_scr, res)
        put_rows(need_scr, [kf - c for c in count_ge([r + 1 for r in res])])

    m_scr[...] = jnp.full(m_scr.shape, jnp.finfo(F32).min, F32)
    acc_scr[...] = jnp.zeros(acc_scr.shape, F32)
    lane = lax.broadcasted_iota(I32, (qb, LANES), 1)
    low_half = lane < HEAD_DIM
    q = q_ref[0]
    zero_q = jnp.zeros((qb, LANES), BF16)
    qm = []
    for h in range(N_HEADS):
        qp = q[:, (h // 2) * LANES:(h // 2 + 1) * LANES]
        qm.append(jnp.where(low_half if h % 2 == 0 else jnp.logical_not(low_half), qp, zero_q))
    thr = jnp.concatenate([pltpu.bitcast(pltpu.bitcast(t_scr[...], F32).T, I32)] * ncol, axis=1)
    need = jnp.concatenate([need_scr[...].T] * ncol, axis=1)
    ut = ut_ref[...]

    def attn_tile(j, run):
        off = pl.multiple_of(j * kt, kt)
        key = keyq_scr[:, pl.ds(off, kt)]
        eq = key == thr
        cnt = _dot(jnp.where(eq, 1.0, 0.0).astype(BF16), ut) + run
        sel = jnp.logical_or(key > thr, jnp.logical_and(eq, cnt <= need))
        bias = jnp.where(sel, 0.0, neg_inf)
        for h in range(N_HEADS):
            p2 = h // 2
            kp = k_ref[0, pl.ds(off, kt), p2 * LANES:(p2 + 1) * LANES]
            vh = v_ref[0, pl.ds(off, kt), h * LANES:(h + 1) * LANES]
            s = _dot_nt(qm[h], kp) + bias
            m_old = m_scr[h]
            m_new = jnp.maximum(m_old, jnp.broadcast_to(jnp.max(s, axis=1, keepdims=True), (qb, LANES)))
            alpha = jnp.exp(m_old - m_new)
            p = jnp.exp(s - jnp.concatenate([m_new] * ncol, axis=1))
            acc_scr[h] = alpha * acc_scr[h] + _dot(p.astype(BF16), vh)
            m_scr[h] = m_new
        return jnp.broadcast_to(cnt[:, kt - 1:kt], (qb, kt))

    lax.fori_loop(0, i + 1, attn_tile, jnp.zeros((qb, kt), F32))

    def head_out(h):
        a = acc_scr[h]
        return a / jnp.broadcast_to(a[:, HEAD_DIM:HEAD_DIM + 1], (qb, LANES))

    for p2 in range(npair):
        hi = pltpu.roll(head_out(2 * p2 + 1), HEAD_DIM, 1)
        o_ref[0, :, p2 * LANES:(p2 + 1) * LANES] = jnp.where(low_half, head_out(2 * p2), hi).astype(BF16)


def _attn(q, k, v, qi, ki, wit, ut, *, qb):
    b, s, aw = q.shape
    topk = min(IDX_TOPK, s // 4)
    assert topk == qb and s % qb == 0 and qb % CHUNK == 0
    rowq = lambda width: pl.BlockSpec((1, qb, width), lambda bi, i: (bi, i, 0))
    allk = lambda width: pl.BlockSpec((1, s, width), lambda bi, i: (bi, 0, 0))
    return pl.pallas_call(
        functools.partial(_attn_kernel, qb=qb, topk=topk),
        out_shape=jax.ShapeDtypeStruct((b, s, aw), BF16),
        grid=(b, s // qb),
        in_specs=[rowq(aw), allk(aw), allk(v.shape[2]), rowq(qi.shape[2]), allk(LANES),
                  pl.BlockSpec((1, SUBLANES, qb), lambda bi, i: (bi, 0, i)),
                  pl.BlockSpec(ut.shape, lambda bi, i: (0, 0))],
        out_specs=rowq(aw),
        scratch_shapes=[pltpu.VMEM((s + qb, qb), I32), pltpu.VMEM((qb, s), I32), pltpu.VMEM((LANES, qb), I32),
                        pltpu.VMEM((LANES, qb), F32),
                        pltpu.VMEM((N_HEADS, qb, LANES), F32), pltpu.VMEM((N_HEADS, qb, LANES), F32)],
        compiler_params=_params(("arbitrary", "arbitrary")),
        name="attn",
    )(q, k, v, qi, ki, wit, ut)


def _rows_to_tiles(tile_ref, x, rows):
    for sl in range(x.shape[1] // LANES):
        tile_ref[pl.ds(sl, rows, stride=SUBLANES), :] = x[:, sl * LANES:(sl + 1) * LANES]


def _tiles_to_rows(tile_ref, rows, lead=()):
    return jnp.concatenate(
        [tile_ref[lead + (pl.ds(sl, rows, stride=SUBLANES), slice(None))] for sl in range(SUBLANES)], axis=1)


def _tile_rows(t):
    return pl.ds(pl.multiple_of(t * SUBLANES, SUBLANES), SUBLANES)


def _merge_kernel(x_ref, ya_ref, yb_ref, sh1_ref, sc1_ref, g1_ref, sh2_ref, sc2_ref, n1_ref, n2_ref,
                  wg_ref, pa_ref, pb_ref, wo_ref, wr_ref, br_ref, tri_ref,
                  x1_ref, h2_ref, meta_ref, cnt_ref, carry, *, tm, d):
    first = jnp.logical_and(pl.program_id(0) == 0, pl.program_id(1) == 0)

    @pl.when(first)
    def _():
        carry[...] = jnp.zeros(carry.shape, F32)

    x = x_ref[0]
    hb = _rms_mod(x, n1_ref[...], sc1_ref[0], sh1_ref[0]).astype(BF16)
    ga = jax.nn.sigmoid(_dot(hb, wg_ref[:, 0:d]))
    gbm = jax.nn.sigmoid(_dot(hb, wg_ref[:, d:2 * d]))
    merged = ga * _dot(ya_ref[0], pa_ref[...]) + gbm * _dot(yb_ref[0], pb_ref[...])
    x1 = x + g1_ref[0] * _dot(merged.astype(BF16), wo_ref[...])
    x1_ref[0] = x1
    h2 = _rms_mod(x1, n2_ref[...], sc2_ref[0], sh2_ref[0])
    _rows_to_tiles(h2_ref, h2, tm)

    lane = lax.broadcasted_iota(I32, (tm, LANES), 1).astype(F32)
    logits = _dot(h2.astype(BF16), wr_ref[...]) + br_ref[...]
    logits = jnp.where(lane < N_EXPERTS, logits, -jnp.inf)
    vals, idxs = [], []
    hot = jnp.zeros((tm, LANES), F32)
    for _ in range(TOP_K):
        mx = jnp.max(logits, axis=1, keepdims=True)
        ix = jnp.min(jnp.where(logits == mx, lane, float(LANES)), axis=1, keepdims=True)
        pick = lane == ix
        hot = jnp.where(pick, 1.0, hot)
        logits = jnp.where(pick, -jnp.inf, logits)
        vals.append(mx)
        idxs.append(ix)
    es = [jnp.exp(vv - vals[0]) for vv in vals]
    den = es[0] + es[1] + es[2] + es[3]

    prefix = _dot(tri_ref[...], hot.astype(BF16)) + carry[0:1, :]
    carry[...] = jnp.broadcast_to(prefix[tm - 1:tm, :] + hot[tm - 1:tm, :], carry.shape)
    cnt_ref[...] = carry[...]
    meta = jnp.zeros((tm, LANES), F32)
    for kk in range(TOP_K):
        rank = jnp.sum(jnp.where(lane == idxs[kk], prefix, 0.0), axis=1, keepdims=True)
        meta = jnp.where(lane == float(kk), idxs[kk], meta)
        meta = jnp.where(lane == float(TOP_K + kk), rank, meta)
        meta = jnp.where(lane == float(2 * TOP_K + kk), es[kk] / den, meta)
    meta_ref[...] = meta


def _merge(x, ya, yb, mods, n1, n2, w_gate, pa, pb, wo, wr, br, tri, *, tm):
    b, s, d = x.shape
    t = b * s
    sh1, sc1, g1, sh2, sc2 = mods
    nst = s // tm
    row = lambda width: pl.BlockSpec((1, tm, width), lambda bi, si: (bi, si, 0))
    mod = pl.BlockSpec((1, 1, d), lambda bi, si: (bi, 0, 0))
    full = lambda a: pl.BlockSpec(a.shape, lambda bi, si: (0,) * a.ndim)
    outs = [jax.ShapeDtypeStruct((b, s, d), F32),
            jax.ShapeDtypeStruct((t * (d // LANES), LANES), F32),
            jax.ShapeDtypeStruct((t, LANES), F32),
            jax.ShapeDtypeStruct((SUBLANES, LANES), F32)]
    return pl.pallas_call(
        functools.partial(_merge_kernel, tm=tm, d=d),
        out_shape=outs,
        grid=(b, nst),
        in_specs=[row(d), row(ya.shape[2]), row(yb.shape[2]), mod, mod, mod, mod, mod, full(n1), full(n2),
                  full(w_gate), full(pa), full(pb), full(wo), full(wr), full(br), full(tri)],
        out_specs=[row(d),
                   pl.BlockSpec((tm * (d // LANES), LANES), lambda bi, si: (bi * nst + si, 0)),
                   pl.BlockSpec((tm, LANES), lambda bi, si: (bi * nst + si, 0)),
                   pl.BlockSpec((SUBLANES, LANES), lambda bi, si: (0, 0))],
        scratch_shapes=[pltpu.VMEM((SUBLANES, LANES), F32)],
        compiler_params=_params(("arbitrary", "arbitrary")),
        name="merge",
    )(x, ya, yb, sh1, sc1, g1, sh2, sc2, n1, n2, w_gate, pa, pb, wo, wr, br, tri)


def _plan_kernel(meta_ref, cnt_ref, tri_ref, dest_ref, be_ref, pad_ref, *, tm, nbp):
    lane_i = lax.broadcasted_iota(I32, (tm, LANES), 1)
    lane = lane_i.astype(F32)
    nblk = jnp.floor((cnt_ref[...] + float(EXPERT_BLOCK - 1)) * (1.0 / EXPERT_BLOCK))
    pend = _dot(nblk.astype(BF16), tri_ref[...])
    pstart = ((pend - nblk) * float(EXPERT_BLOCK))[0:1, :]
    meta = meta_ref[...]
    dest = jnp.zeros((tm, LANES), F32)
    for kk in range(TOP_K):
        ix = meta[:, kk:kk + 1]
        rank = meta[:, TOP_K + kk:TOP_K + kk + 1]
        base = jnp.sum(jnp.where(lane == ix, pstart, 0.0), axis=1, keepdims=True)
        dest = jnp.where(lane == float(kk), base + rank, dest)
    dest_ref[...] = dest.astype(I32)

    blk = lax.broadcasted_iota(I32, (nbp, LANES), 0).astype(F32)
    lane_b = lax.broadcasted_iota(I32, (nbp, LANES), 1)
    ends = jnp.where(lane_b < N_EXPERTS, pend[0:1, :], float(2 ** 30))
    e_of = jnp.sum(jnp.where(ends <= blk, 1.0, 0.0), axis=1, keepdims=True)
    e_of = jnp.minimum(e_of, float(N_EXPERTS - 1))
    used = jnp.sum(jnp.where(lane_b == N_EXPERTS - 1, pend[0:1, :], 0.0), axis=1, keepdims=True)
    be = jnp.where(lane_b == 0, jnp.broadcast_to(e_of, (nbp, LANES)), jnp.broadcast_to(used, (nbp, LANES)))
    be_ref[...] = be.astype(I32)

    cnt = cnt_ref[0:1, :]
    sub = lax.broadcasted_iota(I32, (SUBLANES, LANES), 0)
    pad_lo = jnp.broadcast_to(pstart + cnt, (SUBLANES, LANES))
    pad_len = jnp.broadcast_to(nblk[0:1, :] * float(EXPERT_BLOCK) - cnt, (SUBLANES, LANES))
    pad_ref[...] = jnp.where(sub == 0, pad_lo, pad_len).astype(I32)


def _plan(meta, counts, tri_e, *, tm, nbp):
    t = meta.shape[0]
    return pl.pallas_call(
        functools.partial(_plan_kernel, tm=tm, nbp=nbp),
        out_shape=[jax.ShapeDtypeStruct((t, LANES), I32), jax.ShapeDtypeStruct((nbp, LANES), I32),
                   jax.ShapeDtypeStruct((SUBLANES, LANES), I32)],
        grid=(t // tm,),
        in_specs=[pl.BlockSpec((tm, LANES), lambda i: (i, 0)),
                  pl.BlockSpec((SUBLANES, LANES), lambda i: (0, 0)),
                  pl.BlockSpec((LANES, LANES), lambda i: (0, 0))],
        out_specs=[pl.BlockSpec((tm, LANES), lambda i: (i, 0)),
                   pl.BlockSpec((nbp, LANES), lambda i: (0, 0)),
                   pl.BlockSpec((SUBLANES, LANES), lambda i: (0, 0))],
        compiler_params=_params(("arbitrary",)),
        name="plan",
    )(meta, counts, tri_e)


def _dispatch_kernel(dest_ref, padlo_ref, padlen_ref, used_ref, h_ref, xs_ref, zbuf, sem, zsem, *, tb, nblocks):
    chunks = [EXPERT_BLOCK >> k for k in range(1, EXPERT_BLOCK.bit_length())]

    @pl.when(pl.program_id(0) == 0)
    def _():
        zbuf[...] = jnp.zeros(zbuf.shape, F32)

        def pad_copy(e, rows):
            n = padlen_ref[e]
            off = padlo_ref[e] + (n // (2 * rows)) * (2 * rows)
            return pltpu.make_async_copy(zbuf.at[pl.ds(0, rows * SUBLANES)],
                                         xs_ref.at[pl.ds(pl.multiple_of(off * SUBLANES, SUBLANES), rows * SUBLANES)],
                                         zsem)

        def each_chunk(act):
            def per_expert(e, c):
                for rows in chunks:
                    @pl.when((padlen_ref[e] & rows) != 0)
                    def _():
                        act(pad_copy(e, rows))
                return c
            lax.fori_loop(0, N_EXPERTS, per_expert, 0)

        def tail_copy(b, half):
            off = pl.multiple_of((b * EXPERT_BLOCK + half * (EXPERT_BLOCK // 2)) * SUBLANES, SUBLANES)
            return pltpu.make_async_copy(zbuf, xs_ref.at[pl.ds(off, zbuf.shape[0])], zsem)

        def each_tail(act):
            def per_block(b, c):
                for half in range(2):
                    act(tail_copy(b, half))
                return c
            lax.fori_loop(used_ref[0], nblocks, per_block, 0)

        each_chunk(lambda cp: cp.start())
        each_tail(lambda cp: cp.start())
        each_chunk(lambda cp: cp.wait())
        each_tail(lambda cp: cp.wait())

    def row_copy(t, kk):
        return pltpu.make_async_copy(h_ref.at[_tile_rows(t)], xs_ref.at[_tile_rows(dest_ref[t * TOP_K + kk])], sem)

    def issue(t, c):
        for kk in range(TOP_K):
            row_copy(t, kk).start(priority=kk % 2)
        return c

    lax.fori_loop(0, tb, issue, 0, unroll=4)

    def drain(t, c):
        for kk in range(TOP_K):
            row_copy(t, kk).wait()
        return c

    lax.fori_loop(0, tb, drain, 0, unroll=8)


def _dispatch(dest_flat, pad_lo, pad_len, used, h2, cap, *, tb):
    t = h2.shape[0] // SUBLANES
    return pl.pallas_call(
        functools.partial(_dispatch_kernel, tb=tb, nblocks=cap // EXPERT_BLOCK),
        out_shape=jax.ShapeDtypeStruct((cap * SUBLANES, LANES), F32),
        grid=(t // tb,),
        in_specs=[pl.BlockSpec((tb * TOP_K,), lambda i: (i,), memory_space=pltpu.SMEM),
                  pl.BlockSpec(memory_space=pltpu.SMEM),
                  pl.BlockSpec(memory_space=pltpu.SMEM),
                  pl.BlockSpec(memory_space=pltpu.SMEM),
                  pl.BlockSpec((tb * SUBLANES, LANES), lambda i: (i, 0))],
        out_specs=pl.BlockSpec(memory_space=pl.ANY),
        scratch_shapes=[pltpu.VMEM((EXPERT_BLOCK // 2 * SUBLANES, LANES), F32),
                        pltpu.SemaphoreType.DMA(()), pltpu.SemaphoreType.DMA(())],
        compiler_params=_params(("arbitrary",)),
        name="dispatch",
    )(dest_flat, pad_lo, pad_len, used, h2)


def _ffn_kernel(be_ref, used_ref, xs_ref, wgu_ref, bgu_ref, wd_ref, bd_ref, ys_ref, wgu_bf, wd_bf, last_e,
                *, d, dff):
    i = pl.program_id(0)
    e = be_ref[i]

    @pl.when(i == 0)
    def _():
        last_e[0] = -1

    @pl.when(i < used_ref[0])
    def _():
        @pl.when(last_e[0] != e)
        def _():
            wgu_bf[...] = wgu_ref[0].astype(BF16)
            wd_bf[...] = wd_ref[0].astype(BF16)
            last_e[0] = e

        xb = _tiles_to_rows(xs_ref, EXPERT_BLOCK).astype(BF16)
        gu = _dot(xb, wgu_bf[...]) + bgu_ref[0]
        x_glu = jnp.minimum(gu[:, 0:dff], SWIGLU_LIMIT)
        x_lin = jnp.clip(gu[:, dff:2 * dff], -SWIGLU_LIMIT, SWIGLU_LIMIT)
        act = x_glu * jax.nn.sigmoid(SWIGLU_ALPHA * x_glu) * (x_lin + 1.0)
        y = _dot(act.astype(BF16), wd_bf[...]) + bd_ref[0]
        _rows_to_tiles(ys_ref, y, EXPERT_BLOCK)

    @pl.when(i >= used_ref[0])
    def _():
        ys_ref[...] = jnp.zeros(ys_ref.shape, F32)


def _ffn(block_e, used, xs, w_gate_up, b_gate_up, w_down, b_down):
    ne, d, dff2 = w_gate_up.shape
    dff = dff2 // 2
    blk = EXPERT_BLOCK * SUBLANES
    nblocks = xs.shape[0] // blk
    grid_spec = pltpu.PrefetchScalarGridSpec(
        num_scalar_prefetch=2,
        grid=(nblocks,),
        in_specs=[pl.BlockSpec((blk, LANES), lambda i, be, us: (jnp.minimum(i, us[0] - 1), 0)),
                  pl.BlockSpec((1, d, dff2), lambda i, be, us: (be[i], 0, 0)),
                  pl.BlockSpec((1, 1, dff2), lambda i, be, us: (be[i], 0, 0)),
                  pl.BlockSpec((1, dff, d), lambda i, be, us: (be[i], 0, 0)),
                  pl.BlockSpec((1, 1, d), lambda i, be, us: (be[i], 0, 0))],
        out_specs=pl.BlockSpec((blk, LANES), lambda i, be, us: (i, 0)),
        scratch_shapes=[pltpu.VMEM((d, dff2), BF16), pltpu.VMEM((dff, d), BF16), pltpu.SMEM((1,), I32)])
    return pl.pallas_call(
        functools.partial(_ffn_kernel, d=d, dff=dff),
        out_shape=jax.ShapeDtypeStruct(xs.shape, F32),
        grid_spec=grid_spec,
        compiler_params=_params(("arbitrary",)),
        name="ffn",
    )(block_e, used, xs, w_gate_up, b_gate_up.reshape(ne, 1, dff2), w_down, b_down.reshape(ne, 1, d))


def _combine_kernel(dest_ref, dest_next_ref, ys_ref, x1_ref, g2_ref, meta_ref, o_ref, buf, sems, *, tb, nsteps):
    step = pl.program_id(0) * pl.num_programs(1) + pl.program_id(1)
    slot = step % 2

    def row_copy(idx_ref, sl, t, kk):
        return pltpu.make_async_copy(ys_ref.at[_tile_rows(idx_ref[t * TOP_K + kk])],
                                     buf.at[sl, kk, _tile_rows(t)], sems.at[sl])

    def gather(idx_ref, sl):
        def issue(t, c):
            for kk in range(TOP_K):
                row_copy(idx_ref, sl, t, kk).start(priority=kk % 2)
            return c
        lax.fori_loop(0, tb, issue, 0, unroll=4)

    @pl.when(step == 0)
    def _():
        gather(dest_ref, 0)

    @pl.when(step + 1 < nsteps)
    def _():
        gather(dest_next_ref, 1 - slot)

    def drain(t, c):
        for kk in range(TOP_K):
            row_copy(dest_ref, slot, t, kk).wait()
        return c

    lax.fori_loop(0, tb, drain, 0, unroll=8)

    meta = meta_ref[...]
    moe = None
    for kk in range(TOP_K):
        yk = _tiles_to_rows(buf, tb, lead=(slot, kk))
        term = yk * meta[:, 2 * TOP_K + kk:2 * TOP_K + kk + 1]
        moe = term if moe is None else moe + term
    o_ref[0] = x1_ref[0] + g2_ref[0] * moe


def _combine(dest_flat, ys, x1, g2, meta, *, tb):
    b, s, d = x1.shape
    nst = s // tb
    nsteps = b * nst
    return pl.pallas_call(
        functools.partial(_combine_kernel, tb=tb, nsteps=nsteps),
        out_shape=jax.ShapeDtypeStruct((b, s, d), F32),
        grid=(b, nst),
        in_specs=[pl.BlockSpec((tb * TOP_K,), lambda bi, si: (bi * nst + si,), memory_space=pltpu.SMEM),
                  pl.BlockSpec((tb * TOP_K,), lambda bi, si: (jnp.minimum(bi * nst + si + 1, nsteps - 1),),
                               memory_space=pltpu.SMEM),
                  pl.BlockSpec(memory_space=pl.ANY),
                  pl.BlockSpec((1, tb, d), lambda bi, si: (bi, si, 0)),
                  pl.BlockSpec((1, 1, d), lambda bi, si: (bi, 0, 0)),
                  pl.BlockSpec((tb, LANES), lambda bi, si: (bi * nst + si, 0))],
        out_specs=pl.BlockSpec((1, tb, d), lambda bi, si: (bi, si, 0)),
        scratch_shapes=[pltpu.VMEM((2, TOP_K, tb * SUBLANES, LANES), F32), pltpu.SemaphoreType.DMA((2,))],
        compiler_params=_params(("arbitrary", "arbitrary")),
        name="combine",
    )(dest_flat, dest_flat, ys, x1, g2, meta)


def _tri(n, strict, upper):
    r = lax.broadcasted_iota(I32, (n, n), 0)
    c = lax.broadcasted_iota(I32, (n, n), 1)
    if upper:
        m = (r < c) if strict else (r <= c)
    else:
        m = (c < r) if strict else (c <= r)
    return m.astype(BF16)


def _layer(x, c_pad, w_ada, b_ada, norm1_g, w_in, conv_w, q_norm_g, k_norm_g, w_conv_out, w_attn_out, w_o,
           norm2_g, w_router, b_router, w_gate_up, b_gate_up, w_down, b_down):
    b, s, d = x.shape
    assert d == SUBLANES * LANES
    t = b * s
    cw = conv_w.shape[1]
    aw = N_HEADS * HEAD_DIM
    tm = min(512, s)
    qb = min(IDX_TOPK, s // 4)
    tb = 256

    mod = _ada(c_pad, w_ada, b_ada)[:b]
    sh1, sc1, g1, sh2, sc2, g2 = [m.reshape(b, 1, d) for m in jnp.split(mod, 6, axis=-1)]

    o = 3 * cw + 3 * aw
    w_qi = w_in[:, o:o + IDX_HEADS * IDX_DIM].reshape(d, IDX_HEADS, IDX_DIM)
    w_qi = jnp.pad(w_qi, ((0, 0), (0, 0), (0, LANES - IDX_DIM))).reshape(d, IDX_HEADS * LANES)
    o2 = o + IDX_HEADS * IDX_DIM
    w_ki = jnp.pad(w_in[:, o2:o2 + IDX_DIM], ((0, 0), (0, LANES - IDX_DIM)))
    o3 = o2 + IDX_DIM
    w_wi = jnp.pad(w_in[:, o3:o3 + IDX_HEADS], ((0, 0), (0, LANES - IDX_HEADS)))
    o4 = o3 + IDX_HEADS
    ov = 3 * cw + 2 * aw
    w_v = jnp.pad(w_in[:, ov:ov + aw].reshape(d, N_HEADS, HEAD_DIM), ((0, 0), (0, 0), (0, LANES - HEAD_DIM)))
    vone = (lax.broadcasted_iota(I32, (1, N_HEADS * LANES), 1) % LANES == HEAD_DIM).astype(F32)
    w_main = jnp.concatenate([w_in[:, :ov], w_v.reshape(d, N_HEADS * LANES), w_qi, w_ki, w_wi], axis=1).astype(BF16)
    w_gate = w_in[:, o4:o4 + 2 * d].astype(BF16)
    seg = lax.broadcasted_iota(I32, (aw, aw), 0) // HEAD_DIM == lax.broadcasted_iota(I32, (aw, aw), 1) // HEAD_DIM
    bd = jnp.where(seg, 1.0 / HEAD_DIM, 0.0).astype(BF16)
    qg = jnp.tile(q_norm_g, N_HEADS).reshape(1, aw)
    kg = jnp.tile(k_norm_g, N_HEADS).reshape(1, aw)

    ya, q, k, v, qi, ki, wi = _inproj(x, sh1, sc1, norm1_g.reshape(1, d), w_main, conv_w, qg, kg, bd, vone, tm=tm)
    yb = _attn(q, k, v, qi, ki, wi, _tri(qb, strict=False, upper=True), qb=qb)

    wr = jnp.pad(w_router, ((0, 0), (0, LANES - N_EXPERTS))).astype(BF16)
    br = jnp.pad(b_router, (0, LANES - N_EXPERTS)).reshape(1, LANES)
    x1, h2, meta, counts = _merge(
        x, ya, yb, (sh1, sc1, g1, sh2, sc2), norm1_g.reshape(1, d), norm2_g.reshape(1, d), w_gate,
        w_conv_out.astype(BF16), w_attn_out.astype(BF16), w_o.astype(BF16), wr, br,
        _tri(tm, strict=True, upper=False), tm=tm)

    nk = t * TOP_K
    cap = -(-nk // EXPERT_BLOCK) * EXPERT_BLOCK + N_EXPERTS * EXPERT_BLOCK
    nblocks = cap // EXPERT_BLOCK
    nbp = -(-nblocks // SUBLANES) * SUBLANES
    dest_pad, be_pad, pad = _plan(meta, counts, _tri(LANES, strict=False, upper=True), tm=min(4 * tm, t), nbp=nbp)
    dest_flat = dest_pad[:, :TOP_K].reshape(-1)
    block_e = be_pad[:nblocks, 0]
    used = be_pad[0:1, 1]

    xs = _dispatch(dest_flat, pad[0, :N_EXPERTS], pad[1, :N_EXPERTS], used, h2, cap, tb=tb)
    ys = _ffn(block_e, used, xs, w_gate_up, b_gate_up, w_down, b_down)
    return _combine(dest_flat, ys, x1, g2, meta, tb=tb)


def kernel(x, c, w_ada, b_ada, norm1_g, w_in, conv_w, q_norm_g, k_norm_g, w_conv_out, w_attn_out, w_o, norm2_g,
           w_router, b_router, w_gate_up, b_gate_up, w_down, b_down):
    c_pad = jnp.pad(c, ((0, -c.shape[0] % SUBLANES), (0, 0)))
    for l in range(w_ada.shape[0]):
        x = _layer(x, c_pad, w_ada[l], b_ada[l], norm1_g[l], w_in[l], conv_w[l], q_norm_g[l], k_norm_g[l],
                   w_conv_out[l], w_attn_out[l], w_o[l], norm2_g[l], w_router[l], b_router[l],
                   w_gate_up[l], b_gate_up[l], w_down[l], b_down[l])
    return x
```

```python
import functools

import jax
import jax.numpy as jnp
from jax import lax
from jax.experimental import pallas as pl
from jax.experimental.pallas import tpu as pltpu

F32 = jnp.float32
BF16 = jnp.bfloat16
I32 = jnp.int32

LANES = 128
SUBLANES = 8

CHUNK = 64
CONV_K = 3
N_HEADS = 8
HEAD_DIM = 64
IDX_HEADS = 4
IDX_DIM = 64
IDX_TOPK = 256
N_EXPERTS = 32
TOP_K = 4
SWIGLU_LIMIT = 7.0
SWIGLU_ALPHA = 1.702
EXPERT_BLOCK = 256
EPS = 1e-6

INT_MIN = -(2 ** 31)
LOGIT_SHIFT_MAX = 30.0
VMEM_LIMIT = 56 * 1024 * 1024


def _dot(a, b):
    return jnp.dot(a, b, preferred_element_type=F32)


def _dot_nt(a, b):
    return lax.dot_general(a, b, (((1,), (1,)), ((), ())), preferred_element_type=F32)


def _params(sem, vmem=VMEM_LIMIT):
    return pltpu.CompilerParams(dimension_semantics=sem, vmem_limit_bytes=vmem)


def _ada_kernel(c_ref, w_ref, b_ref, o_ref):
    c = c_ref[...]
    s = c * jax.nn.sigmoid(c)
    o_ref[...] = _dot(s.astype(BF16), w_ref[...].astype(BF16)) + b_ref[...]


def _ada(c_pad, w_ada, b_ada):
    rows, d = c_pad.shape
    n = w_ada.shape[1]
    tn = n // 6
    return pl.pallas_call(
        _ada_kernel,
        out_shape=jax.ShapeDtypeStruct((rows, n), F32),
        grid=(n // tn,),
        in_specs=[pl.BlockSpec((rows, d), lambda j: (0, 0)),
                  pl.BlockSpec((d, tn), lambda j: (0, j)),
                  pl.BlockSpec((1, tn), lambda j: (0, j))],
        out_specs=pl.BlockSpec((rows, tn), lambda j: (0, j)),
        compiler_params=_params(("arbitrary",)),
        name="ada",
    )(c_pad, w_ada, b_ada.reshape(1, n))


def _rms_mod(x, g, sc, sh):
    ms = jnp.mean(x * x, axis=-1, keepdims=True)
    y = x * lax.rsqrt(ms + EPS) * g
    return y * (1.0 + sc) + sh


def _head_rms(t, bd, g):
    sq = t * t
    hi = sq.astype(BF16)
    lo = (sq - hi.astype(F32)).astype(BF16)
    ms = _dot(hi, bd) + _dot(lo, bd)
    return t * lax.rsqrt(ms + EPS) * g


def _inproj_kernel(x_ref, sh_ref, sc_ref, g_ref, w_ref, cw_ref, qg_ref, kg_ref, bd_ref, vone_ref,
                   ya_ref, q_ref, k_ref, v_ref, qi_ref, ki_ref, wit_ref, ubuf, *, tm, cw, aw):
    s = pl.program_id(1)
    hb = _rms_mod(x_ref[0], g_ref[...], sc_ref[0], sh_ref[0]).astype(BF16)

    zc = _dot(hb, w_ref[:, 0:3 * cw])
    xin, gb, gc = zc[:, 0:cw], zc[:, cw:2 * cw], zc[:, 2 * cw:3 * cw]
    u = gc * xin

    @pl.when(s == 0)
    def _():
        ubuf[0:SUBLANES, :] = jnp.zeros((SUBLANES, cw), F32)

    ubuf[SUBLANES:SUBLANES + tm, :] = u
    u1 = ubuf[SUBLANES - 1:SUBLANES - 1 + tm, :]
    u2 = ubuf[SUBLANES - 2:SUBLANES - 2 + tm, :]
    conv = cw_ref[0:1, :] * u2 + cw_ref[1:2, :] * u1 + cw_ref[2:3, :] * u
    ya_ref[0] = (gb * conv).astype(BF16)
    ubuf[0:SUBLANES, :] = ubuf[tm:tm + SUBLANES, :]

    o = 3 * cw
    bd = bd_ref[...]
    q = _dot(hb, w_ref[:, o:o + aw])
    q_ref[0] = (_head_rms(q, bd, qg_ref[...]) * (HEAD_DIM ** -0.5)).astype(BF16)
    k = _dot(hb, w_ref[:, o + aw:o + 2 * aw])
    k_ref[0] = _head_rms(k, bd, kg_ref[...]).astype(BF16)
    vw = N_HEADS * LANES
    v_ref[0] = (_dot(hb, w_ref[:, o + 2 * aw:o + 2 * aw + vw]) + vone_ref[...]).astype(BF16)

    o = o + 2 * aw + vw
    qw = IDX_HEADS * LANES
    qi_ref[0] = _dot(hb, w_ref[:, o:o + qw]).astype(BF16)
    ki_ref[0] = _dot(hb, w_ref[:, o + qw:o + qw + LANES]).astype(BF16)
    wi = _dot(hb, w_ref[:, o + qw + LANES:o + qw + 2 * LANES]) * ((IDX_HEADS * IDX_DIM) ** -0.5)
    wit_ref[0] = wi.T[0:SUBLANES, :]


def _inproj(x, sh1, sc1, g1n, w_main, conv_w, qg, kg, bd, vone, *, tm):
    b, s, d = x.shape
    cw = conv_w.shape[1]
    aw = N_HEADS * HEAD_DIM
    vw = N_HEADS * LANES
    qw = IDX_HEADS * LANES
    row = lambda width: pl.BlockSpec((1, tm, width), lambda bi, si: (bi, si, 0))
    mod = pl.BlockSpec((1, 1, d), lambda bi, si: (bi, 0, 0))
    full = lambda a: pl.BlockSpec(a.shape, lambda bi, si: (0,) * a.ndim)
    outs = [jax.ShapeDtypeStruct((b, s, cw), BF16)] + [jax.ShapeDtypeStruct((b, s, aw), BF16)] * 2 + [
        jax.ShapeDtypeStruct((b, s, vw), BF16),
        jax.ShapeDtypeStruct((b, s, qw), BF16), jax.ShapeDtypeStruct((b, s, LANES), BF16),
        jax.ShapeDtypeStruct((b, SUBLANES, s), F32)]
    return pl.pallas_call(
        functools.partial(_inproj_kernel, tm=tm, cw=cw, aw=aw),
        out_shape=outs,
        grid=(b, s // tm),
        in_specs=[row(d), mod, mod, full(g1n), full(w_main), full(conv_w), full(qg), full(kg), full(bd), full(vone)],
        out_specs=[row(cw), row(aw), row(aw), row(vw), row(qw), row(LANES),
                   pl.BlockSpec((1, SUBLANES, tm), lambda bi, si: (bi, 0, si))],
        scratch_shapes=[pltpu.VMEM((tm + SUBLANES, cw), F32)],
        compiler_params=_params(("arbitrary", "arbitrary")),
        name="inproj",
    )(x, sh1, sc1, g1n, w_main, conv_w, qg, kg, bd, vone)


def _sort_key(score):
    bits = pltpu.bitcast(score, I32)
    return jnp.where(bits < 0, INT_MIN - bits, bits)


def _attn_kernel(bound_ref, q_ref, k_ref, v_ref, qi_ref, ki_ref, wit_ref, ut_ref, o_ref,
                 key_scr, keyq_scr, t_scr, need_scr, m_scr, acc_scr, *, qb, topk, online):
    i = pl.program_id(1)
    kt = qb
    ncol = kt // LANES
    nhalf = qb // LANES
    npair = N_HEADS // 2
    neg_inf = -jnp.inf
    slab = 4 * SUBLANES

    qi = qi_ref[0]
    q4 = jnp.concatenate([qi[:, h * LANES:(h + 1) * LANES] for h in range(IDX_HEADS)], axis=0)
    wit = wit_ref[0]
    chunk_end = ((lax.broadcasted_iota(I32, (1, qb), 1) + i * qb) // CHUNK + 1) * CHUNK

    def score_tile(j):
        off = pl.multiple_of(j * kt, kt)
        dots = _dot_nt(ki_ref[0, pl.ds(off, kt), :], q4)
        sc = wit[0:1, :] * jnp.maximum(dots[:, 0:qb], 0.0)
        for h in range(1, IDX_HEADS):
            sc = sc + wit[h:h + 1, :] * jnp.maximum(dots[:, h * qb:(h + 1) * qb], 0.0)
        kpos = lax.broadcasted_iota(I32, (kt, qb), 0) + j * kt
        key = jnp.where(kpos < chunk_end, _sort_key(sc), INT_MIN)
        key_scr[pl.ds(off, kt), :] = key
        keyq_scr[:, pl.ds(off, kt)] = pltpu.bitcast(pltpu.bitcast(key, F32).T, I32)

    def score_pair(jj, carry):
        score_tile(2 * jj)
        score_tile(2 * jj + 1)
        return carry

    lax.fori_loop(0, (i + 2) // 2, score_pair, 0)

    def count_ge(cands):
        wide = [jnp.broadcast_to(c, (slab, LANES)) for c in cands]

        def body(jj, accs):
            off = pl.multiple_of(jj * (2 * kt), 2 * kt)
            accs = list(accs)
            for r in range(2 * kt // slab):
                for g in range(nhalf):
                    kk = key_scr[pl.ds(off + r * slab, slab), g * LANES:(g + 1) * LANES]
                    accs[g] = accs[g] + jnp.where(kk >= wide[g], 1.0, 0.0)
            return tuple(accs)

        accs = lax.fori_loop(0, (i + 2) // 2, body, tuple(jnp.zeros((slab, LANES), F32) for _ in range(nhalf)))
        return [jnp.sum(a, axis=0, keepdims=True) for a in accs]

    def put_rows(ref, vals):
        for g in range(nhalf):
            ref[:, g * LANES:(g + 1) * LANES] = jnp.broadcast_to(vals[g], (LANES, LANES))

    @pl.when(i == 0)
    def _():
        t_scr[...] = jnp.full(t_scr.shape, INT_MIN + 1, I32)
        need_scr[...] = jnp.full(need_scr.shape, float(topk), F32)

    @pl.when(i > 0)
    def _():
        kf = float(topk)
        zero = jnp.zeros((1, LANES), I32)
        start = tuple(jnp.where(c >= kf, zero, jnp.full((1, LANES), INT_MIN, I32)) for c in count_ge([zero] * nhalf))

        def bit_step(it, res):
            inc = lax.shift_left(jnp.int32(1), jnp.int32(30) - it)
            cands = [r + inc for r in res]
            return tuple(jnp.where(c >= kf, ca, r) for c, ca, r in zip(count_ge(cands), cands, res))

        res = lax.fori_loop(0, 31, bit_step, start)
        put_rows(t_scr, res)
        put_rows(need_scr, [kf - c for c in count_ge([r + 1 for r in res])])

    if online:
        m_scr[...] = jnp.full(m_scr.shape, jnp.finfo(F32).min, F32)
        masked_in = 0.0
    else:
        masked_in = -bound_ref[0]
    acc_scr[...] = jnp.zeros(acc_scr.shape, F32)
    lane = lax.broadcasted_iota(I32, (qb, LANES), 1)
    low_half = lane < HEAD_DIM
    q = q_ref[0]
    zero_q = jnp.zeros((qb, LANES), BF16)
    qm = []
    for h in range(N_HEADS):
        qp = q[:, (h // 2) * LANES:(h // 2 + 1) * LANES]
        qm.append(jnp.where(low_half if h % 2 == 0 else jnp.logical_not(low_half), qp, zero_q))
    qm2 = [jnp.concatenate([qm[2 * p2], qm[2 * p2 + 1]], axis=0) for p2 in range(npair)]
    thr = jnp.concatenate([pltpu.bitcast(pltpu.bitcast(t_scr[...], F32).T, I32)] * ncol, axis=1)
    need = jnp.concatenate([need_scr[...].T] * ncol, axis=1)
    ut = ut_ref[...]

    def attn_tile(j, run):
        off = pl.multiple_of(j * kt, kt)
        key = keyq_scr[:, pl.ds(off, kt)]
        eq = key == thr
        cnt = _dot(jnp.where(eq, 1.0, 0.0).astype(BF16), ut) + run
        sel = jnp.logical_or(key > thr, jnp.logical_and(eq, cnt <= need))
        bias = jnp.where(sel, masked_in, neg_inf)
        for h in range(N_HEADS):
            p2 = h // 2
            if h % 2 == 0:
                s_pair = _dot_nt(qm2[p2], k_ref[0, pl.ds(off, kt), p2 * LANES:(p2 + 1) * LANES])
            vh = v_ref[0, pl.ds(off, kt), h * LANES:(h + 1) * LANES]
            s = s_pair[(h % 2) * qb:(h % 2 + 1) * qb] + bias
            if online:
                m_old = m_scr[h]
                m_new = jnp.maximum(m_old, jnp.broadcast_to(jnp.max(s, axis=1, keepdims=True), (qb, LANES)))
                alpha = jnp.exp(m_old - m_new)
                p = jnp.exp(s - jnp.concatenate([m_new] * ncol, axis=1))
                acc_scr[h] = alpha * acc_scr[h] + _dot(p.astype(BF16), vh)
                m_scr[h] = m_new
            else:
                acc_scr[h] += _dot(jnp.exp(s).astype(BF16), vh)
        return jnp.broadcast_to(cnt[:, kt - 1:kt], (qb, kt))

    lax.fori_loop(0, (i + 2) // 2, lambda jj, run: attn_tile(2 * jj + 1, attn_tile(2 * jj, run)),
                  jnp.zeros((qb, kt), F32))

    def head_out(h):
        a = acc_scr[h]
        return a / jnp.broadcast_to(a[:, HEAD_DIM:HEAD_DIM + 1], (qb, LANES))

    for p2 in range(npair):
        hi = pltpu.roll(head_out(2 * p2 + 1), HEAD_DIM, 1)
        o_ref[0, :, p2 * LANES:(p2 + 1) * LANES] = jnp.where(low_half, head_out(2 * p2), hi).astype(BF16)


def _attn(bound, q, k, v, qi, ki, wit, ut, *, qb, online):
    b, s, aw = q.shape
    topk = min(IDX_TOPK, s // 4)
    assert topk == qb and s % (2 * qb) == 0 and qb % CHUNK == 0
    rowq = lambda width: pl.BlockSpec((1, qb, width), lambda bi, i: (bi, i, 0))
    allk = lambda width: pl.BlockSpec((1, s, width), lambda bi, i: (bi, 0, 0))
    return pl.pallas_call(
        functools.partial(_attn_kernel, qb=qb, topk=topk, online=online),
        out_shape=jax.ShapeDtypeStruct((b, s, aw), BF16),
        grid=(b, s // qb),
        in_specs=[pl.BlockSpec(memory_space=pltpu.SMEM),
                  rowq(aw), allk(aw), allk(v.shape[2]), rowq(qi.shape[2]), allk(LANES),
                  pl.BlockSpec((1, SUBLANES, qb), lambda bi, i: (bi, 0, i)),
                  pl.BlockSpec(ut.shape, lambda bi, i: (0, 0))],
        out_specs=rowq(aw),
        scratch_shapes=[pltpu.VMEM((s + qb, qb), I32), pltpu.VMEM((qb, s), I32), pltpu.VMEM((LANES, qb), I32),
                        pltpu.VMEM((LANES, qb), F32),
                        pltpu.VMEM((N_HEADS, qb, LANES), F32), pltpu.VMEM((N_HEADS, qb, LANES), F32)],
        compiler_params=_params(("arbitrary", "arbitrary")),
        name="attn",
    )(bound, q, k, v, qi, ki, wit, ut)


def _rows_to_tiles(tile_ref, x, rows):
    for sl in range(x.shape[1] // LANES):
        tile_ref[pl.ds(sl, rows, stride=SUBLANES), :] = x[:, sl * LANES:(sl + 1) * LANES]


def _tiles_to_rows(tile_ref, rows, lead=()):
    return jnp.concatenate(
        [tile_ref[lead + (pl.ds(sl, rows, stride=SUBLANES), slice(None))] for sl in range(SUBLANES)], axis=1)


def _tile_rows(t):
    return pl.ds(pl.multiple_of(t * SUBLANES, SUBLANES), SUBLANES)


def _merge_kernel(x_ref, ya_ref, yb_ref, sh1_ref, sc1_ref, g1_ref, sh2_ref, sc2_ref, n1_ref, n2_ref,
                  wg_ref, pa_ref, pb_ref, wo_ref, wr_ref, br_ref, tri_ref,
                  x1_ref, h2_ref, meta_ref, cnt_ref, carry, *, tm, d):
    first = jnp.logical_and(pl.program_id(0) == 0, pl.program_id(1) == 0)

    @pl.when(first)
    def _():
        carry[...] = jnp.zeros(carry.shape, F32)

    x = x_ref[0]
    hb = _rms_mod(x, n1_ref[...], sc1_ref[0], sh1_ref[0]).astype(BF16)
    ga = jax.nn.sigmoid(_dot(hb, wg_ref[:, 0:d]))
    gbm = jax.nn.sigmoid(_dot(hb, wg_ref[:, d:2 * d]))
    merged = ga * _dot(ya_ref[0], pa_ref[...]) + gbm * _dot(yb_ref[0], pb_ref[...])
    x1 = x + g1_ref[0] * _dot(merged.astype(BF16), wo_ref[...])
    x1_ref[0] = x1
    h2 = _rms_mod(x1, n2_ref[...], sc2_ref[0], sh2_ref[0])
    _rows_to_tiles(h2_ref, h2, tm)

    lane = lax.broadcasted_iota(I32, (tm, LANES), 1).astype(F32)
    logits = _dot(h2.astype(BF16), wr_ref[...]) + br_ref[...]
    logits = jnp.where(lane < N_EXPERTS, logits, -jnp.inf)
    vals, idxs = [], []
    hot = jnp.zeros((tm, LANES), F32)
    for _ in range(TOP_K):
        mx = jnp.max(logits, axis=1, keepdims=True)
        ix = jnp.min(jnp.where(logits == mx, lane, float(LANES)), axis=1, keepdims=True)
        pick = lane == ix
        hot = jnp.where(pick, 1.0, hot)
        logits = jnp.where(pick, -jnp.inf, logits)
        vals.append(mx)
        idxs.append(ix)
    es = [jnp.exp(vv - vals[0]) for vv in vals]
    den = es[0] + es[1] + es[2] + es[3]

    prefix = _dot(tri_ref[...], hot.astype(BF16)) + carry[0:1, :]
    carry[...] = jnp.broadcast_to(prefix[tm - 1:tm, :] + hot[tm - 1:tm, :], carry.shape)
    cnt_ref[...] = carry[...]
    meta = jnp.zeros((tm, LANES), F32)
    for kk in range(TOP_K):
        rank = jnp.sum(jnp.where(lane == idxs[kk], prefix, 0.0), axis=1, keepdims=True)
        meta = jnp.where(lane == float(kk), idxs[kk], meta)
        meta = jnp.where(lane == float(TOP_K + kk), rank, meta)
        meta = jnp.where(lane == float(2 * TOP_K + kk), es[kk] / den, meta)
    meta_ref[...] = meta


def _merge(x, ya, yb, mods, n1, n2, w_gate, pa, pb, wo, wr, br, tri, *, tm):
    b, s, d = x.shape
    t = b * s
    sh1, sc1, g1, sh2, sc2 = mods
    nst = s // tm
    row = lambda width: pl.BlockSpec((1, tm, width), lambda bi, si: (bi, si, 0))
    mod = pl.BlockSpec((1, 1, d), lambda bi, si: (bi, 0, 0))
    full = lambda a: pl.BlockSpec(a.shape, lambda bi, si: (0,) * a.ndim)
    outs = [jax.ShapeDtypeStruct((b, s, d), F32),
            jax.ShapeDtypeStruct((t * (d // LANES), LANES), F32),
            jax.ShapeDtypeStruct((t, LANES), F32),
            jax.ShapeDtypeStruct((SUBLANES, LANES), F32)]
    return pl.pallas_call(
        functools.partial(_merge_kernel, tm=tm, d=d),
        out_shape=outs,
        grid=(b, nst),
        in_specs=[row(d), row(ya.shape[2]), row(yb.shape[2]), mod, mod, mod, mod, mod, full(n1), full(n2),
                  full(w_gate), full(pa), full(pb), full(wo), full(wr), full(br), full(tri)],
        out_specs=[row(d),
                   pl.BlockSpec((tm * (d // LANES), LANES), lambda bi, si: (bi * nst + si, 0)),
                   pl.BlockSpec((tm, LANES), lambda bi, si: (bi * nst + si, 0)),
                   pl.BlockSpec((SUBLANES, LANES), lambda bi, si: (0, 0))],
        scratch_shapes=[pltpu.VMEM((SUBLANES, LANES), F32)],
        compiler_params=_params(("arbitrary", "arbitrary")),
        name="merge",
    )(x, ya, yb, sh1, sc1, g1, sh2, sc2, n1, n2, w_gate, pa, pb, wo, wr, br, tri)


def _plan_kernel(meta_ref, cnt_ref, tri_ref, dest_ref, be_ref, pad_ref, *, tm, nbp):
    lane_i = lax.broadcasted_iota(I32, (tm, LANES), 1)
    lane = lane_i.astype(F32)
    nblk = jnp.floor((cnt_ref[...] + float(EXPERT_BLOCK - 1)) * (1.0 / EXPERT_BLOCK))
    pend = _dot(nblk.astype(BF16), tri_ref[...])
    pstart = ((pend - nblk) * float(EXPERT_BLOCK))[0:1, :]
    meta = meta_ref[...]
    dest = jnp.zeros((tm, LANES), F32)
    for kk in range(TOP_K):
        ix = meta[:, kk:kk + 1]
        rank = meta[:, TOP_K + kk:TOP_K + kk + 1]
        base = jnp.sum(jnp.where(lane == ix, pstart, 0.0), axis=1, keepdims=True)
        dest = jnp.where(lane == float(kk), base + rank, dest)
    dest_ref[...] = dest.astype(I32)

    blk = lax.broadcasted_iota(I32, (nbp, LANES), 0).astype(F32)
    lane_b = lax.broadcasted_iota(I32, (nbp, LANES), 1)
    ends = jnp.where(lane_b < N_EXPERTS, pend[0:1, :], float(2 ** 30))
    e_of = jnp.sum(jnp.where(ends <= blk, 1.0, 0.0), axis=1, keepdims=True)
    e_of = jnp.minimum(e_of, float(N_EXPERTS - 1))
    used = jnp.sum(jnp.where(lane_b == N_EXPERTS - 1, pend[0:1, :], 0.0), axis=1, keepdims=True)
    be = jnp.where(lane_b == 0, jnp.broadcast_to(e_of, (nbp, LANES)), jnp.broadcast_to(used, (nbp, LANES)))
    be_ref[...] = be.astype(I32)

    cnt = cnt_ref[0:1, :]
    sub = lax.broadcasted_iota(I32, (SUBLANES, LANES), 0)
    pad_lo = jnp.broadcast_to(pstart + cnt, (SUBLANES, LANES))
    pad_len = jnp.broadcast_to(nblk[0:1, :] * float(EXPERT_BLOCK) - cnt, (SUBLANES, LANES))
    pad_ref[...] = jnp.where(sub == 0, pad_lo, pad_len).astype(I32)


def _plan(meta, counts, tri_e, *, tm, nbp):
    t = meta.shape[0]
    return pl.pallas_call(
        functools.partial(_plan_kernel, tm=tm, nbp=nbp),
        out_shape=[jax.ShapeDtypeStruct((t, LANES), I32), jax.ShapeDtypeStruct((nbp, LANES), I32),
                   jax.ShapeDtypeStruct((SUBLANES, LANES), I32)],
        grid=(t // tm,),
        in_specs=[pl.BlockSpec((tm, LANES), lambda i: (i, 0)),
                  pl.BlockSpec((SUBLANES, LANES), lambda i: (0, 0)),
                  pl.BlockSpec((LANES, LANES), lambda i: (0, 0))],
        out_specs=[pl.BlockSpec((tm, LANES), lambda i: (i, 0)),
                   pl.BlockSpec((nbp, LANES), lambda i: (0, 0)),
                   pl.BlockSpec((SUBLANES, LANES), lambda i: (0, 0))],
        compiler_params=_params(("arbitrary",)),
        name="plan",
    )(meta, counts, tri_e)


def _dispatch_kernel(dest_ref, padlo_ref, padlen_ref, used_ref, h_ref, xs_ref, zbuf, sem, zsem, *, tb, nblocks):
    chunks = [EXPERT_BLOCK >> k for k in range(1, EXPERT_BLOCK.bit_length())]

    @pl.when(pl.program_id(0) == 0)
    def _():
        zbuf[...] = jnp.zeros(zbuf.shape, F32)

        def pad_copy(e, rows):
            n = padlen_ref[e]
            off = padlo_ref[e] + (n // (2 * rows)) * (2 * rows)
            return pltpu.make_async_copy(zbuf.at[pl.ds(0, rows * SUBLANES)],
                                         xs_ref.at[pl.ds(pl.multiple_of(off * SUBLANES, SUBLANES), rows * SUBLANES)],
                                         zsem)

        def each_chunk(act):
            def per_expert(e, c):
                for rows in chunks:
                    @pl.when((padlen_ref[e] & rows) != 0)
                    def _():
                        act(pad_copy(e, rows))
                return c
            lax.fori_loop(0, N_EXPERTS, per_expert, 0)

        def tail_copy(b, half):
            off = pl.multiple_of((b * EXPERT_BLOCK + half * (EXPERT_BLOCK // 2)) * SUBLANES, SUBLANES)
            return pltpu.make_async_copy(zbuf, xs_ref.at[pl.ds(off, zbuf.shape[0])], zsem)

        def each_tail(act):
            def per_block(b, c):
                for half in range(2):
                    act(tail_copy(b, half))
                return c
            lax.fori_loop(used_ref[0], nblocks, per_block, 0)

        each_chunk(lambda cp: cp.start())
        each_tail(lambda cp: cp.start())
        each_chunk(lambda cp: cp.wait())
        each_tail(lambda cp: cp.wait())

    def row_copy(t, kk):
        return pltpu.make_async_copy(h_ref.at[_tile_rows(t)], xs_ref.at[_tile_rows(dest_ref[t * TOP_K + kk])], sem)

    def issue(t, c):
        for kk in range(TOP_K):
            row_copy(t, kk).start(priority=kk % 2)
        return c

    lax.fori_loop(0, tb, issue, 0, unroll=4)

    def drain(t, c):
        for kk in range(TOP_K):
            row_copy(t, kk).wait()
        return c

    lax.fori_loop(0, tb, drain, 0, unroll=8)


def _dispatch(dest_flat, pad_lo, pad_len, used, h2, cap, *, tb):
    t = h2.shape[0] // SUBLANES
    return pl.pallas_call(
        functools.partial(_dispatch_kernel, tb=tb, nblocks=cap // EXPERT_BLOCK),
        out_shape=jax.ShapeDtypeStruct((cap * SUBLANES, LANES), F32),
        grid=(t // tb,),
        in_specs=[pl.BlockSpec((tb * TOP_K,), lambda i: (i,), memory_space=pltpu.SMEM),
                  pl.BlockSpec(memory_space=pltpu.SMEM),
                  pl.BlockSpec(memory_space=pltpu.SMEM),
                  pl.BlockSpec(memory_space=pltpu.SMEM),
                  pl.BlockSpec((tb * SUBLANES, LANES), lambda i: (i, 0))],
        out_specs=pl.BlockSpec(memory_space=pl.ANY),
        scratch_shapes=[pltpu.VMEM((EXPERT_BLOCK // 2 * SUBLANES, LANES), F32),
                        pltpu.SemaphoreType.DMA(()), pltpu.SemaphoreType.DMA(())],
        compiler_params=_params(("arbitrary",)),
        name="dispatch",
    )(dest_flat, pad_lo, pad_len, used, h2)


def _ffn_kernel(be_ref, used_ref, nxt_ref, xs_ref, wgu_hbm, bgu_ref, wd_hbm, bd_ref, ys_ref,
                wgu_f32, wd_f32, wgu_bf, wd_bf, state, sems, *, d, dff):
    i = pl.program_id(0)
    e = be_ref[i]

    def fetch(expert, slot):
        return (pltpu.make_async_copy(wgu_hbm.at[expert], wgu_f32.at[slot], sems.at[0, slot]),
                pltpu.make_async_copy(wd_hbm.at[expert], wd_f32.at[slot], sems.at[1, slot]))

    @pl.when(i == 0)
    def _():
        state[0] = -1
        state[1] = 1
        for cp in fetch(e, 0):
            cp.start()

    @pl.when(i < used_ref[0])
    def _():
        @pl.when(state[0] != e)
        def _():
            slot = 1 - state[1]
            for cp in fetch(e, slot):
                cp.wait()
            wgu_bf[...] = wgu_f32[slot].astype(BF16)
            wd_bf[...] = wd_f32[slot].astype(BF16)
            state[0] = e
            state[1] = slot

            @pl.when(nxt_ref[i] >= 0)
            def _():
                for cp in fetch(nxt_ref[i], 1 - slot):
                    cp.start()

        xb = _tiles_to_rows(xs_ref, EXPERT_BLOCK).astype(BF16)
        gu = _dot(xb, wgu_bf[...]) + bgu_ref[0]
        x_glu = jnp.minimum(gu[:, 0:dff], SWIGLU_LIMIT)
        x_lin = jnp.clip(gu[:, dff:2 * dff], -SWIGLU_LIMIT, SWIGLU_LIMIT)
        act = x_glu * jax.nn.sigmoid(SWIGLU_ALPHA * x_glu) * (x_lin + 1.0)
        y = _dot(act.astype(BF16), wd_bf[...]) + bd_ref[0]
        _rows_to_tiles(ys_ref, y, EXPERT_BLOCK)

    @pl.when(i >= used_ref[0])
    def _():
        ys_ref[...] = jnp.zeros(ys_ref.shape, F32)


def _ffn(block_e, used, next_e, xs, w_gate_up, b_gate_up, w_down, b_down):
    ne, d, dff2 = w_gate_up.shape
    dff = dff2 // 2
    blk = EXPERT_BLOCK * SUBLANES
    nblocks = xs.shape[0] // blk
    grid_spec = pltpu.PrefetchScalarGridSpec(
        num_scalar_prefetch=3,
        grid=(nblocks,),
        in_specs=[pl.BlockSpec((blk, LANES), lambda i, be, us, nx: (jnp.minimum(i, us[0] - 1), 0)),
                  pl.BlockSpec(memory_space=pl.ANY),
                  pl.BlockSpec((1, 1, dff2), lambda i, be, us, nx: (be[i], 0, 0)),
                  pl.BlockSpec(memory_space=pl.ANY),
                  pl.BlockSpec((1, 1, d), lambda i, be, us, nx: (be[i], 0, 0))],
        out_specs=pl.BlockSpec((blk, LANES), lambda i, be, us, nx: (i, 0)),
        scratch_shapes=[pltpu.VMEM((2, d, dff2), F32), pltpu.VMEM((2, dff, d), F32),
                        pltpu.VMEM((d, dff2), BF16), pltpu.VMEM((dff, d), BF16),
                        pltpu.SMEM((2,), I32), pltpu.SemaphoreType.DMA((2, 2))])
    return pl.pallas_call(
        functools.partial(_ffn_kernel, d=d, dff=dff),
        out_shape=jax.ShapeDtypeStruct(xs.shape, F32),
        grid_spec=grid_spec,
        compiler_params=_params(("arbitrary",)),
        name="ffn",
    )(block_e, used, next_e, xs, w_gate_up, b_gate_up.reshape(ne, 1, dff2), w_down, b_down.reshape(ne, 1, d))


def _combine_kernel(dest_ref, dest_next_ref, ys_ref, x1_ref, g2_ref, meta_ref, o_ref, buf, sems, *, tb, nsteps):
    step = pl.program_id(0) * pl.num_programs(1) + pl.program_id(1)
    slot = step % 2

    def row_copy(idx_ref, sl, t, kk):
        return pltpu.make_async_copy(ys_ref.at[_tile_rows(idx_ref[t * TOP_K + kk])],
                                     buf.at[sl, kk, _tile_rows(t)], sems.at[sl])

    def gather(idx_ref, sl):
        def issue(t, c):
            for kk in range(TOP_K):
                row_copy(idx_ref, sl, t, kk).start(priority=kk % 2)
            return c
        lax.fori_loop(0, tb, issue, 0, unroll=8)

    @pl.when(step == 0)
    def _():
        gather(dest_ref, 0)

    @pl.when(step + 1 < nsteps)
    def _():
        gather(dest_next_ref, 1 - slot)

    def drain(t, c):
        for kk in range(TOP_K):
            row_copy(dest_ref, slot, t, kk).wait()
        return c

    lax.fori_loop(0, tb, drain, 0, unroll=8)

    meta = meta_ref[...]
    moe = None
    for kk in range(TOP_K):
        yk = _tiles_to_rows(buf, tb, lead=(slot, kk))
        term = yk * meta[:, 2 * TOP_K + kk:2 * TOP_K + kk + 1]
        moe = term if moe is None else moe + term
    o_ref[0] = x1_ref[0] + g2_ref[0] * moe


def _combine(dest_flat, ys, x1, g2, meta, *, tb):
    b, s, d = x1.shape
    nst = s // tb
    nsteps = b * nst
    return pl.pallas_call(
        functools.partial(_combine_kernel, tb=tb, nsteps=nsteps),
        out_shape=jax.ShapeDtypeStruct((b, s, d), F32),
        grid=(b, nst),
        in_specs=[pl.BlockSpec((tb * TOP_K,), lambda bi, si: (bi * nst + si,), memory_space=pltpu.SMEM),
                  pl.BlockSpec((tb * TOP_K,), lambda bi, si: (jnp.minimum(bi * nst + si + 1, nsteps - 1),),
                               memory_space=pltpu.SMEM),
                  pl.BlockSpec(memory_space=pl.ANY),
                  pl.BlockSpec((1, tb, d), lambda bi, si: (bi, si, 0)),
                  pl.BlockSpec((1, 1, d), lambda bi, si: (bi, 0, 0)),
                  pl.BlockSpec((tb, LANES), lambda bi, si: (bi * nst + si, 0))],
        out_specs=pl.BlockSpec((1, tb, d), lambda bi, si: (bi, si, 0)),
        scratch_shapes=[pltpu.VMEM((2, TOP_K, tb * SUBLANES, LANES), F32), pltpu.SemaphoreType.DMA((2,))],
        compiler_params=_params(("arbitrary", "arbitrary")),
        name="combine",
    )(dest_flat, dest_flat, ys, x1, g2, meta)


def _tri(n, strict, upper):
    r = lax.broadcasted_iota(I32, (n, n), 0)
    c = lax.broadcasted_iota(I32, (n, n), 1)
    if upper:
        m = (r < c) if strict else (r <= c)
    else:
        m = (c < r) if strict else (c <= r)
    return m.astype(BF16)


def _layer(x, c_pad, w_ada, b_ada, norm1_g, w_in, conv_w, q_norm_g, k_norm_g, w_conv_out, w_attn_out, w_o,
           norm2_g, w_router, b_router, w_gate_up, b_gate_up, w_down, b_down):
    b, s, d = x.shape
    assert d == SUBLANES * LANES
    t = b * s
    cw = conv_w.shape[1]
    aw = N_HEADS * HEAD_DIM
    tm = min(512, s)
    qb = min(IDX_TOPK, s // 4)
    tb = 256

    mod = _ada(c_pad, w_ada, b_ada)[:b]
    sh1, sc1, g1, sh2, sc2, g2 = [m.reshape(b, 1, d) for m in jnp.split(mod, 6, axis=-1)]

    o = 3 * cw + 3 * aw
    w_qi = w_in[:, o:o + IDX_HEADS * IDX_DIM].reshape(d, IDX_HEADS, IDX_DIM)
    w_qi = jnp.pad(w_qi, ((0, 0), (0, 0), (0, LANES - IDX_DIM))).reshape(d, IDX_HEADS * LANES)
    o2 = o + IDX_HEADS * IDX_DIM
    w_ki = jnp.pad(w_in[:, o2:o2 + IDX_DIM], ((0, 0), (0, LANES - IDX_DIM)))
    o3 = o2 + IDX_DIM
    w_wi = jnp.pad(w_in[:, o3:o3 + IDX_HEADS], ((0, 0), (0, LANES - IDX_HEADS)))
    o4 = o3 + IDX_HEADS
    ov = 3 * cw + 2 * aw
    w_v = jnp.pad(w_in[:, ov:ov + aw].reshape(d, N_HEADS, HEAD_DIM), ((0, 0), (0, 0), (0, LANES - HEAD_DIM)))
    vone = (lax.broadcasted_iota(I32, (1, N_HEADS * LANES), 1) % LANES == HEAD_DIM).astype(F32)
    w_main = jnp.concatenate([w_in[:, :ov], w_v.reshape(d, N_HEADS * LANES), w_qi, w_ki, w_wi], axis=1).astype(BF16)
    w_gate = w_in[:, o4:o4 + 2 * d].astype(BF16)
    seg = lax.broadcasted_iota(I32, (aw, aw), 0) // HEAD_DIM == lax.broadcasted_iota(I32, (aw, aw), 1) // HEAD_DIM
    bd = jnp.where(seg, 1.0 / HEAD_DIM, 0.0).astype(BF16)
    qg = jnp.tile(q_norm_g, N_HEADS).reshape(1, aw)
    kg = jnp.tile(k_norm_g, N_HEADS).reshape(1, aw)

    ya, q, k, v, qi, ki, wi = _inproj(x, sh1, sc1, norm1_g.reshape(1, d), w_main, conv_w, qg, kg, bd, vone, tm=tm)
    bound = (HEAD_DIM ** 0.5 * 1.01 * jnp.max(jnp.abs(q_norm_g)) * jnp.max(jnp.abs(k_norm_g))).reshape(1)
    ut = _tri(qb, strict=False, upper=True)
    yb = lax.cond(bound[0] <= LOGIT_SHIFT_MAX,
                  lambda: _attn(bound, q, k, v, qi, ki, wi, ut, qb=qb, online=False),
                  lambda: _attn(bound, q, k, v, qi, ki, wi, ut, qb=qb, online=True))

    wr = jnp.pad(w_router, ((0, 0), (0, LANES - N_EXPERTS))).astype(BF16)
    br = jnp.pad(b_router, (0, LANES - N_EXPERTS)).reshape(1, LANES)
    x1, h2, meta, counts = _merge(
        x, ya, yb, (sh1, sc1, g1, sh2, sc2), norm1_g.reshape(1, d), norm2_g.reshape(1, d), w_gate,
        w_conv_out.astype(BF16), w_attn_out.astype(BF16), w_o.astype(BF16), wr, br,
        _tri(tm, strict=True, upper=False), tm=tm)

    nk = t * TOP_K
    cap = -(-nk // EXPERT_BLOCK) * EXPERT_BLOCK + N_EXPERTS * EXPERT_BLOCK
    nblocks = cap // EXPERT_BLOCK
    nbp = -(-nblocks // SUBLANES) * SUBLANES
    dest_pad, be_pad, pad = _plan(meta, counts, _tri(LANES, strict=False, upper=True), tm=min(4 * tm, t), nbp=nbp)
    dest_flat = dest_pad[:, :TOP_K].reshape(-1)
    block_e = be_pad[:nblocks, 0]
    used = be_pad[0:1, 1]

    xs = _dispatch(dest_flat, pad[0, :N_EXPERTS], pad[1, :N_EXPERTS], used, h2, cap, tb=tb)
    blk_id = jnp.arange(nblocks, dtype=I32)
    after = jnp.searchsorted(jnp.where(blk_id < used[0], block_e, N_EXPERTS), block_e, side='right').astype(I32)
    next_e = jnp.where(after < used[0], block_e[jnp.minimum(after, nblocks - 1)], -1).astype(I32)
    ys = _ffn(block_e, used, next_e, xs, w_gate_up, b_gate_up, w_down, b_down)
    return _combine(dest_flat, ys, x1, g2, meta, tb=tb)


def kernel(x, c, w_ada, b_ada, norm1_g, w_in, conv_w, q_norm_g, k_norm_g, w_conv_out, w_attn_out, w_o, norm2_g,
           w_router, b_router, w_gate_up, b_gate_up, w_down, b_down):
    c_pad = jnp.pad(c, ((0, -c.shape[0] % SUBLANES), (0, 0)))
    for l in range(w_ada.shape[0]):
        x = _layer(x, c_pad, w_ada[l], b_ada[l], norm1_g[l], w_in[l], conv_w[l], q_norm_g[l], k_norm_g[l],
                   w_conv_out[l], w_attn_out[l], w_o[l], norm2_g[l], w_router[l], b_router[l],
                   w_gate_up[l], b_gate_up[l], w_down[l], b_down[l])
    return x
```

```python
import functools

import jax
import jax.numpy as jnp
from jax import lax
from jax.experimental import pallas as pl
from jax.experimental.pallas import tpu as pltpu

F32 = jnp.float32
BF16 = jnp.bfloat16
I32 = jnp.int32

LANES = 128
SUBLANES = 8

CHUNK = 64
CONV_K = 3
N_HEADS = 8
HEAD_DIM = 64
IDX_HEADS = 4
IDX_DIM = 64
IDX_TOPK = 256
N_EXPERTS = 32
TOP_K = 4
SWIGLU_LIMIT = 7.0
SWIGLU_ALPHA = 1.702
EXPERT_BLOCK = 256
EPS = 1e-6

INT_MIN = -(2 ** 31)
LOGIT_SHIFT_MAX = 30.0
VMEM_LIMIT = 56 * 1024 * 1024


def _dot(a, b):
    return jnp.dot(a, b, preferred_element_type=F32)


def _dot_nt(a, b):
    return lax.dot_general(a, b, (((1,), (1,)), ((), ())), preferred_element_type=F32)


def _params(sem, vmem=VMEM_LIMIT):
    return pltpu.CompilerParams(dimension_semantics=sem, vmem_limit_bytes=vmem)


def _ada_kernel(c_ref, w_ref, b_ref, o_ref):
    c = c_ref[...]
    s = c * jax.nn.sigmoid(c)
    o_ref[...] = _dot(s.astype(BF16), w_ref[...].astype(BF16)) + b_ref[...]


def _ada(c_pad, w_ada, b_ada):
    rows, d = c_pad.shape
    n = w_ada.shape[1]
    tn = n // 6
    return pl.pallas_call(
        _ada_kernel,
        out_shape=jax.ShapeDtypeStruct((rows, n), F32),
        grid=(n // tn,),
        in_specs=[pl.BlockSpec((rows, d), lambda j: (0, 0)),
                  pl.BlockSpec((d, tn), lambda j: (0, j)),
                  pl.BlockSpec((1, tn), lambda j: (0, j))],
        out_specs=pl.BlockSpec((rows, tn), lambda j: (0, j)),
        compiler_params=_params(("arbitrary",)),
        name="ada",
    )(c_pad, w_ada, b_ada.reshape(1, n))


def _rms_mod(x, g, sc, sh):
    ms = jnp.mean(x * x, axis=-1, keepdims=True)
    y = x * lax.rsqrt(ms + EPS) * g
    return y * (1.0 + sc) + sh


def _head_rms(t, bd, g):
    sq = t * t
    hi = sq.astype(BF16)
    lo = (sq - hi.astype(F32)).astype(BF16)
    ms = _dot(hi, bd) + _dot(lo, bd)
    return t * lax.rsqrt(ms + EPS) * g


def _inproj_kernel(x_ref, sh_ref, sc_ref, g_ref, w_ref, cw_ref, qg_ref, kg_ref, bd_ref, vone_ref,
                   ya_ref, q_ref, k_ref, v_ref, qi_ref, ki_ref, wit_ref, ubuf, *, tm, cw, aw):
    s = pl.program_id(1)
    hb = _rms_mod(x_ref[0], g_ref[...], sc_ref[0], sh_ref[0]).astype(BF16)

    zc = _dot(hb, w_ref[:, 0:3 * cw])
    xin, gb, gc = zc[:, 0:cw], zc[:, cw:2 * cw], zc[:, 2 * cw:3 * cw]
    u = gc * xin

    @pl.when(s == 0)
    def _():
        ubuf[0:SUBLANES, :] = jnp.zeros((SUBLANES, cw), F32)

    ubuf[SUBLANES:SUBLANES + tm, :] = u
    u1 = ubuf[SUBLANES - 1:SUBLANES - 1 + tm, :]
    u2 = ubuf[SUBLANES - 2:SUBLANES - 2 + tm, :]
    conv = cw_ref[0:1, :] * u2 + cw_ref[1:2, :] * u1 + cw_ref[2:3, :] * u
    ya_ref[0] = (gb * conv).astype(BF16)
    ubuf[0:SUBLANES, :] = ubuf[tm:tm + SUBLANES, :]

    o = 3 * cw
    bd = bd_ref[...]
    q = _dot(hb, w_ref[:, o:o + aw])
    q_ref[0] = (_head_rms(q, bd, qg_ref[...]) * (HEAD_DIM ** -0.5)).astype(BF16)
    k = _dot(hb, w_ref[:, o + aw:o + 2 * aw])
    k_ref[0] = _head_rms(k, bd, kg_ref[...]).astype(BF16)
    vw = N_HEADS * LANES
    v_ref[0] = (_dot(hb, w_ref[:, o + 2 * aw:o + 2 * aw + vw]) + vone_ref[...]).astype(BF16)

    o = o + 2 * aw + vw
    qw = IDX_HEADS * LANES
    qi_ref[0] = _dot(hb, w_ref[:, o:o + qw]).astype(BF16)
    ki_ref[0] = _dot(hb, w_ref[:, o + qw:o + qw + LANES]).astype(BF16)
    wi = _dot(hb, w_ref[:, o + qw + LANES:o + qw + 2 * LANES]) * ((IDX_HEADS * IDX_DIM) ** -0.5)
    wit_ref[0] = wi.T[0:SUBLANES, :]


def _inproj(x, sh1, sc1, g1n, w_main, conv_w, qg, kg, bd, vone, *, tm):
    b, s, d = x.shape
    cw = conv_w.shape[1]
    aw = N_HEADS * HEAD_DIM
    vw = N_HEADS * LANES
    qw = IDX_HEADS * LANES
    row = lambda width: pl.BlockSpec((1, tm, width), lambda bi, si: (bi, si, 0))
    mod = pl.BlockSpec((1, 1, d), lambda bi, si: (bi, 0, 0))
    full = lambda a: pl.BlockSpec(a.shape, lambda bi, si: (0,) * a.ndim)
    outs = [jax.ShapeDtypeStruct((b, s, cw), BF16)] + [jax.ShapeDtypeStruct((b, s, aw), BF16)] * 2 + [
        jax.ShapeDtypeStruct((b, s, vw), BF16),
        jax.ShapeDtypeStruct((b, s, qw), BF16), jax.ShapeDtypeStruct((b, s, LANES), BF16),
        jax.ShapeDtypeStruct((b, SUBLANES, s), F32)]
    return pl.pallas_call(
        functools.partial(_inproj_kernel, tm=tm, cw=cw, aw=aw),
        out_shape=outs,
        grid=(b, s // tm),
        in_specs=[row(d), mod, mod, full(g1n), full(w_main), full(conv_w), full(qg), full(kg), full(bd), full(vone)],
        out_specs=[row(cw), row(aw), row(aw), row(vw), row(qw), row(LANES),
                   pl.BlockSpec((1, SUBLANES, tm), lambda bi, si: (bi, 0, si))],
        scratch_shapes=[pltpu.VMEM((tm + SUBLANES, cw), F32)],
        compiler_params=_params(("arbitrary", "arbitrary")),
        name="inproj",
    )(x, sh1, sc1, g1n, w_main, conv_w, qg, kg, bd, vone)


def _sort_key(score):
    bits = pltpu.bitcast(score, I32)
    return jnp.where(bits < 0, INT_MIN - bits, bits)


def _attn_kernel(bound_ref, q_ref, k_ref, v_ref, qi_ref, ki_ref, wit_ref, ut_ref, o_ref,
                 key_scr, keyq_scr, t_scr, need_scr, m_scr, acc_scr, *, qb, topk, online):
    i = pl.program_id(1)
    kt = qb
    ncol = kt // LANES
    nhalf = qb // LANES
    npair = N_HEADS // 2
    neg_inf = -jnp.inf
    slab = 4 * SUBLANES

    qi = qi_ref[0]
    q4 = jnp.concatenate([qi[:, h * LANES:(h + 1) * LANES] for h in range(IDX_HEADS)], axis=0)
    wit = wit_ref[0]
    chunk_end = ((lax.broadcasted_iota(I32, (1, qb), 1) + i * qb) // CHUNK + 1) * CHUNK

    def score_tile(j):
        off = pl.multiple_of(j * kt, kt)
        dots = _dot_nt(ki_ref[0, pl.ds(off, kt), :], q4)
        sc = wit[0:1, :] * jnp.maximum(dots[:, 0:qb], 0.0)
        for h in range(1, IDX_HEADS):
            sc = sc + wit[h:h + 1, :] * jnp.maximum(dots[:, h * qb:(h + 1) * qb], 0.0)
        kpos = lax.broadcasted_iota(I32, (kt, qb), 0) + j * kt
        key = jnp.where(kpos < chunk_end, _sort_key(sc), INT_MIN)
        key_scr[pl.ds(off, kt), :] = key
        keyq_scr[:, pl.ds(off, kt)] = pltpu.bitcast(pltpu.bitcast(key, F32).T, I32)

    def score_pair(jj, carry):
        score_tile(2 * jj)
        score_tile(2 * jj + 1)
        return carry

    lax.fori_loop(0, (i + 2) // 2, score_pair, 0)

    def count_ge(cands):
        wide = [jnp.broadcast_to(c, (slab, LANES)) for c in cands]

        def body(jj, accs):
            off = pl.multiple_of(jj * (2 * kt), 2 * kt)
            accs = list(accs)
            for r in range(2 * kt // slab):
                for g in range(nhalf):
                    kk = key_scr[pl.ds(off + r * slab, slab), g * LANES:(g + 1) * LANES]
                    accs[g] = accs[g] + jnp.where(kk >= wide[g], 1.0, 0.0)
            return tuple(accs)

        accs = lax.fori_loop(0, (i + 2) // 2, body, tuple(jnp.zeros((slab, LANES), F32) for _ in range(nhalf)))
        return [jnp.sum(a, axis=0, keepdims=True) for a in accs]

    def put_rows(ref, vals):
        for g in range(nhalf):
            ref[:, g * LANES:(g + 1) * LANES] = jnp.broadcast_to(vals[g], (LANES, LANES))

    @pl.when(i == 0)
    def _():
        t_scr[...] = jnp.full(t_scr.shape, INT_MIN + 1, I32)
        need_scr[...] = jnp.full(need_scr.shape, float(topk), F32)

    @pl.when(i > 0)
    def _():
        kf = float(topk)
        zero = jnp.zeros((1, LANES), I32)
        start = tuple(jnp.where(c >= kf, zero, jnp.full((1, LANES), INT_MIN, I32)) for c in count_ge([zero] * nhalf))

        def bit_step(it, res):
            inc = lax.shift_left(jnp.int32(1), jnp.int32(30) - it)
            cands = [r + inc for r in res]
            return tuple(jnp.where(c >= kf, ca, r) for c, ca, r in zip(count_ge(cands), cands, res))

        res = lax.fori_loop(0, 31, bit_step, start)
        put_rows(t_scr, res)
        put_rows(need_scr, [kf - c for c in count_ge([r + 1 for r in res])])

    if online:
        m_scr[...] = jnp.full(m_scr.shape, jnp.finfo(F32).min, F32)
        masked_in = 0.0
    else:
        masked_in = -bound_ref[0]
    acc_scr[...] = jnp.zeros(acc_scr.shape, F32)
    lane = lax.broadcasted_iota(I32, (qb, LANES), 1)
    low_half = lane < HEAD_DIM
    q = q_ref[0]
    zero_q = jnp.zeros((qb, LANES), BF16)
    qm = []
    for h in range(N_HEADS):
        qp = q[:, (h // 2) * LANES:(h // 2 + 1) * LANES]
        qm.append(jnp.where(low_half if h % 2 == 0 else jnp.logical_not(low_half), qp, zero_q))
    qm2 = [jnp.concatenate([qm[2 * p2], qm[2 * p2 + 1]], axis=0) for p2 in range(npair)]
    thr = jnp.concatenate([pltpu.bitcast(pltpu.bitcast(t_scr[...], F32).T, I32)] * ncol, axis=1)
    need = jnp.concatenate([need_scr[...].T] * ncol, axis=1)
    ut = ut_ref[...]

    def attn_tile(j, run):
        off = pl.multiple_of(j * kt, kt)
        key = keyq_scr[:, pl.ds(off, kt)]
        eq = key == thr
        cnt = _dot(jnp.where(eq, 1.0, 0.0).astype(BF16), ut) + run
        sel = jnp.logical_or(key > thr, jnp.logical_and(eq, cnt <= need))
        bias = jnp.where(sel, masked_in, neg_inf)
        for h in range(N_HEADS):
            p2 = h // 2
            if h % 2 == 0:
                s_pair = _dot_nt(qm2[p2], k_ref[0, pl.ds(off, kt), p2 * LANES:(p2 + 1) * LANES])
            vh = v_ref[0, pl.ds(off, kt), h * LANES:(h + 1) * LANES]
            s = s_pair[(h % 2) * qb:(h % 2 + 1) * qb] + bias
            if online:
                m_old = m_scr[h]
                m_new = jnp.maximum(m_old, jnp.broadcast_to(jnp.max(s, axis=1, keepdims=True), (qb, LANES)))
                alpha = jnp.exp(m_old - m_new)
                p = jnp.exp(s - jnp.concatenate([m_new] * ncol, axis=1))
                acc_scr[h] = alpha * acc_scr[h] + _dot(p.astype(BF16), vh)
                m_scr[h] = m_new
            else:
                acc_scr[h] += _dot(jnp.exp(s).astype(BF16), vh)
        return jnp.broadcast_to(cnt[:, kt - 1:kt], (qb, kt))

    lax.fori_loop(0, (i + 2) // 2, lambda jj, run: attn_tile(2 * jj + 1, attn_tile(2 * jj, run)),
                  jnp.zeros((qb, kt), F32))

    def head_out(h):
        a = acc_scr[h]
        return a / jnp.broadcast_to(a[:, HEAD_DIM:HEAD_DIM + 1], (qb, LANES))

    for p2 in range(npair):
        hi = pltpu.roll(head_out(2 * p2 + 1), HEAD_DIM, 1)
        o_ref[0, :, p2 * LANES:(p2 + 1) * LANES] = jnp.where(low_half, head_out(2 * p2), hi).astype(BF16)


def _attn(bound, q, k, v, qi, ki, wit, ut, *, qb, online):
    b, s, aw = q.shape
    topk = min(IDX_TOPK, s // 4)
    assert topk == qb and s % (2 * qb) == 0 and qb % CHUNK == 0
    rowq = lambda width: pl.BlockSpec((1, qb, width), lambda bi, i: (bi, i, 0))
    allk = lambda width: pl.BlockSpec((1, s, width), lambda bi, i: (bi, 0, 0))
    return pl.pallas_call(
        functools.partial(_attn_kernel, qb=qb, topk=topk, online=online),
        out_shape=jax.ShapeDtypeStruct((b, s, aw), BF16),
        grid=(b, s // qb),
        in_specs=[pl.BlockSpec(memory_space=pltpu.SMEM),
                  rowq(aw), allk(aw), allk(v.shape[2]), rowq(qi.shape[2]), allk(LANES),
                  pl.BlockSpec((1, SUBLANES, qb), lambda bi, i: (bi, 0, i)),
                  pl.BlockSpec(ut.shape, lambda bi, i: (0, 0))],
        out_specs=rowq(aw),
        scratch_shapes=[pltpu.VMEM((s + qb, qb), I32), pltpu.VMEM((qb, s), I32), pltpu.VMEM((LANES, qb), I32),
                        pltpu.VMEM((LANES, qb), F32),
                        pltpu.VMEM((N_HEADS, qb, LANES), F32), pltpu.VMEM((N_HEADS, qb, LANES), F32)],
        compiler_params=_params(("arbitrary", "arbitrary")),
        name="attn",
    )(bound, q, k, v, qi, ki, wit, ut)


def _rows_to_tiles(tile_ref, x, rows):
    for sl in range(x.shape[1] // LANES):
        tile_ref[pl.ds(sl, rows, stride=SUBLANES), :] = x[:, sl * LANES:(sl + 1) * LANES]


def _tiles_to_rows(tile_ref, rows, lead=()):
    return jnp.concatenate(
        [tile_ref[lead + (pl.ds(sl, rows, stride=SUBLANES), slice(None))] for sl in range(SUBLANES)], axis=1)


def _tile_rows(t):
    return pl.ds(pl.multiple_of(t * SUBLANES, SUBLANES), SUBLANES)


def _merge_kernel(x_ref, ya_ref, yb_ref, sh1_ref, sc1_ref, g1_ref, sh2_ref, sc2_ref, n1_ref, n2_ref,
                  wg_ref, pa_ref, pb_ref, wo_ref, wr_ref, br_ref, tri_ref,
                  x1_ref, h2_ref, meta_ref, cnt_ref, carry, *, tm, d):
    first = jnp.logical_and(pl.program_id(0) == 0, pl.program_id(1) == 0)

    @pl.when(first)
    def _():
        carry[...] = jnp.zeros(carry.shape, F32)

    x = x_ref[0]
    hb = _rms_mod(x, n1_ref[...], sc1_ref[0], sh1_ref[0]).astype(BF16)
    ga = jax.nn.sigmoid(_dot(hb, wg_ref[:, 0:d]))
    gbm = jax.nn.sigmoid(_dot(hb, wg_ref[:, d:2 * d]))
    merged = ga * _dot(ya_ref[0], pa_ref[...]) + gbm * _dot(yb_ref[0], pb_ref[...])
    x1 = x + g1_ref[0] * _dot(merged.astype(BF16), wo_ref[...])
    x1_ref[0] = x1
    h2 = _rms_mod(x1, n2_ref[...], sc2_ref[0], sh2_ref[0])
    _rows_to_tiles(h2_ref, h2, tm)

    lane = lax.broadcasted_iota(I32, (tm, LANES), 1).astype(F32)
    logits = _dot(h2.astype(BF16), wr_ref[...]) + br_ref[...]
    logits = jnp.where(lane < N_EXPERTS, logits, -jnp.inf)
    vals, idxs = [], []
    hot = jnp.zeros((tm, LANES), F32)
    for _ in range(TOP_K):
        mx = jnp.max(logits, axis=1, keepdims=True)
        ix = jnp.min(jnp.where(logits == mx, lane, float(LANES)), axis=1, keepdims=True)
        pick = lane == ix
        hot = jnp.where(pick, 1.0, hot)
        logits = jnp.where(pick, -jnp.inf, logits)
        vals.append(mx)
        idxs.append(ix)
    es = [jnp.exp(vv - vals[0]) for vv in vals]
    den = es[0] + es[1] + es[2] + es[3]

    prefix = _dot(tri_ref[...], hot.astype(BF16)) + carry[0:1, :]
    carry[...] = jnp.broadcast_to(prefix[tm - 1:tm, :] + hot[tm - 1:tm, :], carry.shape)
    cnt_ref[...] = carry[...]
    meta = jnp.zeros((tm, LANES), F32)
    for kk in range(TOP_K):
        rank = jnp.sum(jnp.where(lane == idxs[kk], prefix, 0.0), axis=1, keepdims=True)
        meta = jnp.where(lane == float(kk), idxs[kk], meta)
        meta = jnp.where(lane == float(TOP_K + kk), rank, meta)
        meta = jnp.where(lane == float(2 * TOP_K + kk), es[kk] / den, meta)
    meta_ref[...] = meta


def _merge(x, ya, yb, mods, n1, n2, w_gate, pa, pb, wo, wr, br, tri, *, tm):
    b, s, d = x.shape
    t = b * s
    sh1, sc1, g1, sh2, sc2 = mods
    nst = s // tm
    row = lambda width: pl.BlockSpec((1, tm, width), lambda bi, si: (bi, si, 0))
    mod = pl.BlockSpec((1, 1, d), lambda bi, si: (bi, 0, 0))
    full = lambda a: pl.BlockSpec(a.shape, lambda bi, si: (0,) * a.ndim)
    outs = [jax.ShapeDtypeStruct((b, s, d), F32),
            jax.ShapeDtypeStruct((t * (d // LANES), LANES), F32),
            jax.ShapeDtypeStruct((t, LANES), F32),
            jax.ShapeDtypeStruct((SUBLANES, LANES), F32)]
    return pl.pallas_call(
        functools.partial(_merge_kernel, tm=tm, d=d),
        out_shape=outs,
        grid=(b, nst),
        in_specs=[row(d), row(ya.shape[2]), row(yb.shape[2]), mod, mod, mod, mod, mod, full(n1), full(n2),
                  full(w_gate), full(pa), full(pb), full(wo), full(wr), full(br), full(tri)],
        out_specs=[row(d),
                   pl.BlockSpec((tm * (d // LANES), LANES), lambda bi, si: (bi * nst + si, 0)),
                   pl.BlockSpec((tm, LANES), lambda bi, si: (bi * nst + si, 0)),
                   pl.BlockSpec((SUBLANES, LANES), lambda bi, si: (0, 0))],
        scratch_shapes=[pltpu.VMEM((SUBLANES, LANES), F32)],
        compiler_params=_params(("arbitrary", "arbitrary")),
        name="merge",
    )(x, ya, yb, sh1, sc1, g1, sh2, sc2, n1, n2, w_gate, pa, pb, wo, wr, br, tri)


def _plan_kernel(meta_ref, cnt_ref, tri_ref, dest_ref, be_ref, pad_ref, *, tm, nbp):
    lane_i = lax.broadcasted_iota(I32, (tm, LANES), 1)
    lane = lane_i.astype(F32)
    nblk = jnp.floor((cnt_ref[...] + float(EXPERT_BLOCK - 1)) * (1.0 / EXPERT_BLOCK))
    pend = _dot(nblk.astype(BF16), tri_ref[...])
    pstart = ((pend - nblk) * float(EXPERT_BLOCK))[0:1, :]
    meta = meta_ref[...]
    dest = jnp.zeros((tm, LANES), F32)
    for kk in range(TOP_K):
        ix = meta[:, kk:kk + 1]
        rank = meta[:, TOP_K + kk:TOP_K + kk + 1]
        base = jnp.sum(jnp.where(lane == ix, pstart, 0.0), axis=1, keepdims=True)
        dest = jnp.where(lane == float(kk), base + rank, dest)
    dest_ref[...] = dest.astype(I32)

    blk = lax.broadcasted_iota(I32, (nbp, LANES), 0).astype(F32)
    lane_b = lax.broadcasted_iota(I32, (nbp, LANES), 1)
    ends = jnp.where(lane_b < N_EXPERTS, pend[0:1, :], float(2 ** 30))
    e_of = jnp.sum(jnp.where(ends <= blk, 1.0, 0.0), axis=1, keepdims=True)
    e_of = jnp.minimum(e_of, float(N_EXPERTS - 1))
    used = jnp.sum(jnp.where(lane_b == N_EXPERTS - 1, pend[0:1, :], 0.0), axis=1, keepdims=True)
    be = jnp.where(lane_b == 0, jnp.broadcast_to(e_of, (nbp, LANES)), jnp.broadcast_to(used, (nbp, LANES)))
    be_ref[...] = be.astype(I32)

    cnt = cnt_ref[0:1, :]
    sub = lax.broadcasted_iota(I32, (SUBLANES, LANES), 0)
    pad_lo = jnp.broadcast_to(pstart + cnt, (SUBLANES, LANES))
    pad_len = jnp.broadcast_to(nblk[0:1, :] * float(EXPERT_BLOCK) - cnt, (SUBLANES, LANES))
    pad_ref[...] = jnp.where(sub == 0, pad_lo, pad_len).astype(I32)


def _plan(meta, counts, tri_e, *, tm, nbp):
    t = meta.shape[0]
    return pl.pallas_call(
        functools.partial(_plan_kernel, tm=tm, nbp=nbp),
        out_shape=[jax.ShapeDtypeStruct((t, LANES), I32), jax.ShapeDtypeStruct((nbp, LANES), I32),
                   jax.ShapeDtypeStruct((SUBLANES, LANES), I32)],
        grid=(t // tm,),
        in_specs=[pl.BlockSpec((tm, LANES), lambda i: (i, 0)),
                  pl.BlockSpec((SUBLANES, LANES), lambda i: (0, 0)),
                  pl.BlockSpec((LANES, LANES), lambda i: (0, 0))],
        out_specs=[pl.BlockSpec((tm, LANES), lambda i: (i, 0)),
                   pl.BlockSpec((nbp, LANES), lambda i: (0, 0)),
                   pl.BlockSpec((SUBLANES, LANES), lambda i: (0, 0))],
        compiler_params=_params(("arbitrary",)),
        name="plan",
    )(meta, counts, tri_e)


def _dispatch_kernel(dest_ref, padlo_ref, padlen_ref, used_ref, h_ref, xs_ref, zbuf, sem, zsem, *, tb, nblocks):
    chunks = [EXPERT_BLOCK >> k for k in range(1, EXPERT_BLOCK.bit_length())]

    @pl.when(pl.program_id(0) == 0)
    def _():
        zbuf[...] = jnp.zeros(zbuf.shape, F32)

        def pad_copy(e, rows):
            n = padlen_ref[e]
            off = padlo_ref[e] + (n // (2 * rows)) * (2 * rows)
            return pltpu.make_async_copy(zbuf.at[pl.ds(0, rows * SUBLANES)],
                                         xs_ref.at[pl.ds(pl.multiple_of(off * SUBLANES, SUBLANES), rows * SUBLANES)],
                                         zsem)

        def each_chunk(act):
            def per_expert(e, c):
                for rows in chunks:
                    @pl.when((padlen_ref[e] & rows) != 0)
                    def _():
                        act(pad_copy(e, rows))
                return c
            lax.fori_loop(0, N_EXPERTS, per_expert, 0)

        def tail_copy(b, half):
            off = pl.multiple_of((b * EXPERT_BLOCK + half * (EXPERT_BLOCK // 2)) * SUBLANES, SUBLANES)
            return pltpu.make_async_copy(zbuf, xs_ref.at[pl.ds(off, zbuf.shape[0])], zsem)

        def each_tail(act):
            def per_block(b, c):
                for half in range(2):
                    act(tail_copy(b, half))
                return c
            lax.fori_loop(used_ref[0], nblocks, per_block, 0)

        each_chunk(lambda cp: cp.start())
        each_tail(lambda cp: cp.start())
        each_chunk(lambda cp: cp.wait())
        each_tail(lambda cp: cp.wait())

    def row_copy(t, kk):
        return pltpu.make_async_copy(h_ref.at[_tile_rows(t)], xs_ref.at[_tile_rows(dest_ref[t * TOP_K + kk])], sem)

    def issue(t, c):
        for kk in range(TOP_K):
            row_copy(t, kk).start(priority=kk % 2)
        return c

    lax.fori_loop(0, tb, issue, 0, unroll=4)

    def drain(t, c):
        for kk in range(TOP_K):
            row_copy(t, kk).wait()
        return c

    lax.fori_loop(0, tb, drain, 0, unroll=8)


def _dispatch(dest_flat, pad_lo, pad_len, used, h2, cap, *, tb):
    t = h2.shape[0] // SUBLANES
    return pl.pallas_call(
        functools.partial(_dispatch_kernel, tb=tb, nblocks=cap // EXPERT_BLOCK),
        out_shape=jax.ShapeDtypeStruct((cap * SUBLANES, LANES), F32),
        grid=(t // tb,),
        in_specs=[pl.BlockSpec((tb * TOP_K,), lambda i: (i,), memory_space=pltpu.SMEM),
                  pl.BlockSpec(memory_space=pltpu.SMEM),
                  pl.BlockSpec(memory_space=pltpu.SMEM),
                  pl.BlockSpec(memory_space=pltpu.SMEM),
                  pl.BlockSpec((tb * SUBLANES, LANES), lambda i: (i, 0))],
        out_specs=pl.BlockSpec(memory_space=pl.ANY),
        scratch_shapes=[pltpu.VMEM((EXPERT_BLOCK // 2 * SUBLANES, LANES), F32),
                        pltpu.SemaphoreType.DMA(()), pltpu.SemaphoreType.DMA(())],
        compiler_params=_params(("arbitrary",)),
        name="dispatch",
    )(dest_flat, pad_lo, pad_len, used, h2)


def _ffn_kernel(be_ref, used_ref, nxt_ref, xs_ref, wgu_hbm, bgu_ref, wd_hbm, bd_ref, ys_ref,
                wgu_f32, wd_f32, wgu_bf, wd_bf, state, sems, *, d, dff):
    i = pl.program_id(0)
    e = be_ref[i]

    def fetch(expert, slot):
        return (pltpu.make_async_copy(wgu_hbm.at[expert], wgu_f32.at[slot], sems.at[0, slot]),
                pltpu.make_async_copy(wd_hbm.at[expert], wd_f32.at[slot], sems.at[1, slot]))

    @pl.when(i == 0)
    def _():
        state[0] = -1
        state[1] = 1
        for cp in fetch(e, 0):
            cp.start()

    @pl.when(i < used_ref[0])
    def _():
        @pl.when(state[0] != e)
        def _():
            slot = 1 - state[1]
            for cp in fetch(e, slot):
                cp.wait()
            wgu_bf[...] = wgu_f32[slot].astype(BF16)
            wd_bf[...] = wd_f32[slot].astype(BF16)
            state[0] = e
            state[1] = slot

            @pl.when(nxt_ref[i] >= 0)
            def _():
                for cp in fetch(nxt_ref[i], 1 - slot):
                    cp.start()

        xb = _tiles_to_rows(xs_ref, EXPERT_BLOCK).astype(BF16)
        gu = _dot(xb, wgu_bf[...]) + bgu_ref[0]
        x_glu = jnp.minimum(gu[:, 0:dff], SWIGLU_LIMIT)
        x_lin = jnp.clip(gu[:, dff:2 * dff], -SWIGLU_LIMIT, SWIGLU_LIMIT)
        act = x_glu * jax.nn.sigmoid(SWIGLU_ALPHA * x_glu) * (x_lin + 1.0)
        y = _dot(act.astype(BF16), wd_bf[...]) + bd_ref[0]
        _rows_to_tiles(ys_ref, y, EXPERT_BLOCK)

    @pl.when(i >= used_ref[0])
    def _():
        ys_ref[...] = jnp.zeros(ys_ref.shape, F32)


def _ffn(block_e, used, next_e, xs, w_gate_up, b_gate_up, w_down, b_down):
    ne, d, dff2 = w_gate_up.shape
    dff = dff2 // 2
    blk = EXPERT_BLOCK * SUBLANES
    nblocks = xs.shape[0] // blk
    grid_spec = pltpu.PrefetchScalarGridSpec(
        num_scalar_prefetch=3,
        grid=(nblocks,),
        in_specs=[pl.BlockSpec((blk, LANES), lambda i, be, us, nx: (jnp.minimum(i, us[0] - 1), 0)),
                  pl.BlockSpec(memory_space=pl.ANY),
                  pl.BlockSpec((1, 1, dff2), lambda i, be, us, nx: (be[i], 0, 0)),
                  pl.BlockSpec(memory_space=pl.ANY),
                  pl.BlockSpec((1, 1, d), lambda i, be, us, nx: (be[i], 0, 0))],
        out_specs=pl.BlockSpec((blk, LANES), lambda i, be, us, nx: (i, 0)),
        scratch_shapes=[pltpu.VMEM((2, d, dff2), F32), pltpu.VMEM((2, dff, d), F32),
                        pltpu.VMEM((d, dff2), BF16), pltpu.VMEM((dff, d), BF16),
                        pltpu.SMEM((2,), I32), pltpu.SemaphoreType.DMA((2, 2))])
    return pl.pallas_call(
        functools.partial(_ffn_kernel, d=d, dff=dff),
        out_shape=jax.ShapeDtypeStruct(xs.shape, F32),
        grid_spec=grid_spec,
        compiler_params=_params(("arbitrary",)),
        name="ffn",
    )(block_e, used, next_e, xs, w_gate_up, b_gate_up.reshape(ne, 1, dff2), w_down, b_down.reshape(ne, 1, d))


def _combine_kernel(dest_ref, dest_next_ref, ys_ref, x1_ref, g2_ref, meta_ref, o_ref, buf, sems, *, tb, nsteps):
    step = pl.program_id(0) * pl.num_programs(1) + pl.program_id(1)
    slot = step % 2

    def row_copy(idx_ref, sl, t, kk):
        return pltpu.make_async_copy(ys_ref.at[_tile_rows(idx_ref[t * TOP_K + kk])],
                                     buf.at[sl, kk, _tile_rows(t)], sems.at[sl])

    def gather(idx_ref, sl):
        def issue(t, c):
            for kk in range(TOP_K):
                row_copy(idx_ref, sl, t, kk).start(priority=kk % 2)
            return c
        lax.fori_loop(0, tb, issue, 0, unroll=8)

    @pl.when(step == 0)
    def _():
        gather(dest_ref, 0)

    @pl.when(step + 1 < nsteps)
    def _():
        gather(dest_next_ref, 1 - slot)

    def drain(t, c):
        for kk in range(TOP_K):
            row_copy(dest_ref, slot, t, kk).wait()
        return c

    lax.fori_loop(0, tb, drain, 0, unroll=8)

    meta = meta_ref[...]
    moe = None
    for kk in range(TOP_K):
        yk = _tiles_to_rows(buf, tb, lead=(slot, kk))
        term = yk * meta[:, 2 * TOP_K + kk:2 * TOP_K + kk + 1]
        moe = term if moe is None else moe + term
    o_ref[0] = x1_ref[0] + g2_ref[0] * moe


def _combine(dest_flat, ys, x1, g2, meta, *, tb):
    b, s, d = x1.shape
    nst = s // tb
    nsteps = b * nst
    return pl.pallas_call(
        functools.partial(_combine_kernel, tb=tb, nsteps=nsteps),
        out_shape=jax.ShapeDtypeStruct((b, s, d), F32),
        grid=(b, nst),
        in_specs=[pl.BlockSpec((tb * TOP_K,), lambda bi, si: (bi * nst + si,), memory_space=pltpu.SMEM),
                  pl.BlockSpec((tb * TOP_K,), lambda bi, si: (jnp.minimum(bi * nst + si + 1, nsteps - 1),),
                               memory_space=pltpu.SMEM),
                  pl.BlockSpec(memory_space=pl.ANY),
                  pl.BlockSpec((1, tb, d), lambda bi, si: (bi, si, 0)),
                  pl.BlockSpec((1, 1, d), lambda bi, si: (bi, 0, 0)),
                  pl.BlockSpec((tb, LANES), lambda bi, si: (bi * nst + si, 0))],
        out_specs=pl.BlockSpec((1, tb, d), lambda bi, si: (bi, si, 0)),
        scratch_shapes=[pltpu.VMEM((2, TOP_K, tb * SUBLANES, LANES), F32), pltpu.SemaphoreType.DMA((2,))],
        compiler_params=_params(("arbitrary", "arbitrary")),
        name="combine",
    )(dest_flat, dest_flat, ys, x1, g2, meta)


def _tri(n, strict, upper):
    r = lax.broadcasted_iota(I32, (n, n), 0)
    c = lax.broadcasted_iota(I32, (n, n), 1)
    if upper:
        m = (r < c) if strict else (r <= c)
    else:
        m = (c < r) if strict else (c <= r)
    return m.astype(BF16)


def _layer(x, c_pad, w_ada, b_ada, norm1_g, w_in, conv_w, q_norm_g, k_norm_g, w_conv_out, w_attn_out, w_o,
           norm2_g, w_router, b_router, w_gate_up, b_gate_up, w_down, b_down):
    b, s, d = x.shape
    assert d == SUBLANES * LANES
    t = b * s
    cw = conv_w.shape[1]
    aw = N_HEADS * HEAD_DIM
    tm = min(512, s)
    qb = min(IDX_TOPK, s // 4)
    tb = 256

    mod = _ada(c_pad, w_ada, b_ada)[:b]
    sh1, sc1, g1, sh2, sc2, g2 = [m.reshape(b, 1, d) for m in jnp.split(mod, 6, axis=-1)]

    o = 3 * cw + 3 * aw
    w_qi = w_in[:, o:o + IDX_HEADS * IDX_DIM].reshape(d, IDX_HEADS, IDX_DIM)
    w_qi = jnp.pad(w_qi, ((0, 0), (0, 0), (0, LANES - IDX_DIM))).reshape(d, IDX_HEADS * LANES)
    o2 = o + IDX_HEADS * IDX_DIM
    w_ki = jnp.pad(w_in[:, o2:o2 + IDX_DIM], ((0, 0), (0, LANES - IDX_DIM)))
    o3 = o2 + IDX_DIM
    w_wi = jnp.pad(w_in[:, o3:o3 + IDX_HEADS], ((0, 0), (0, LANES - IDX_HEADS)))
    o4 = o3 + IDX_HEADS
    ov = 3 * cw + 2 * aw
    w_v = jnp.pad(w_in[:, ov:ov + aw].reshape(d, N_HEADS, HEAD_DIM), ((0, 0), (0, 0), (0, LANES - HEAD_DIM)))
    vone = (lax.broadcasted_iota(I32, (1, N_HEADS * LANES), 1) % LANES == HEAD_DIM).astype(F32)
    w_main = jnp.concatenate([w_in[:, :ov], w_v.reshape(d, N_HEADS * LANES), w_qi, w_ki, w_wi], axis=1).astype(BF16)
    w_gate = w_in[:, o4:o4 + 2 * d].astype(BF16)
    seg = lax.broadcasted_iota(I32, (aw, aw), 0) // HEAD_DIM == lax.broadcasted_iota(I32, (aw, aw), 1) // HEAD_DIM
    bd = jnp.where(seg, 1.0 / HEAD_DIM, 0.0).astype(BF16)
    qg = jnp.tile(q_norm_g, N_HEADS).reshape(1, aw)
    kg = jnp.tile(k_norm_g, N_HEADS).reshape(1, aw)

    ya, q, k, v, qi, ki, wi = _inproj(x, sh1, sc1, norm1_g.reshape(1, d), w_main, conv_w, qg, kg, bd, vone, tm=tm)
    bound = (HEAD_DIM ** 0.5 * 1.01 * jnp.max(jnp.abs(q_norm_g)) * jnp.max(jnp.abs(k_norm_g))).reshape(1)
    ut = _tri(qb, strict=False, upper=True)
    yb = lax.cond(bound[0] <= LOGIT_SHIFT_MAX,
                  lambda: _attn(bound, q, k, v, qi, ki, wi, ut, qb=qb, online=False),
                  lambda: _attn(bound, q, k, v, qi, ki, wi, ut, qb=qb, online=True))

    wr = jnp.pad(w_router, ((0, 0), (0, LANES - N_EXPERTS))).astype(BF16)
    br = jnp.pad(b_router, (0, LANES - N_EXPERTS)).reshape(1, LANES)
    x1, h2, meta, counts = _merge(
        x, ya, yb, (sh1, sc1, g1, sh2, sc2), norm1_g.reshape(1, d), norm2_g.reshape(1, d), w_gate,
        w_conv_out.astype(BF16), w_attn_out.astype(BF16), w_o.astype(BF16), wr, br,
        _tri(tm, strict=True, upper=False), tm=tm)

    nk = t * TOP_K
    cap = -(-nk // EXPERT_BLOCK) * EXPERT_BLOCK + N_EXPERTS * EXPERT_BLOCK
    nblocks = cap // EXPERT_BLOCK
    nbp = -(-nblocks // SUBLANES) * SUBLANES
    dest_pad, be_pad, pad = _plan(meta, counts, _tri(LANES, strict=False, upper=True), tm=min(4 * tm, t), nbp=nbp)
    dest_flat = dest_pad[:, :TOP_K].reshape(-1)
    block_e = be_pad[:nblocks, 0]
    used = be_pad[0:1, 1]

    xs = _dispatch(dest_flat, pad[0, :N_EXPERTS], pad[1, :N_EXPERTS], used, h2, cap, tb=tb)
    later = jnp.logical_and(block_e[None, :] > block_e[:, None], jnp.arange(nblocks)[None, :] < used[0])
    next_e = jnp.min(jnp.where(later, block_e[None, :], N_EXPERTS), axis=1)
    next_e = jnp.where(next_e < N_EXPERTS, next_e, -1).astype(I32)
    ys = _ffn(block_e, used, next_e, xs, w_gate_up, b_gate_up, w_down, b_down)
    return _combine(dest_flat, ys, x1, g2, meta, tb=tb)


def kernel(x, c, w_ada, b_ada, norm1_g, w_in, conv_w, q_norm_g, k_norm_g, w_conv_out, w_attn_out, w_o, norm2_g,
           w_router, b_router, w_gate_up, b_gate_up, w_down, b_down):
    c_pad = jnp.pad(c, ((0, -c.shape[0] % SUBLANES), (0, 0)))
    for l in range(w_ada.shape[0]):
        x = _layer(x, c_pad, w_ada[l], b_ada[l], norm1_g[l], w_in[l], conv_w[l], q_norm_g[l], k_norm_g[l],
                   w_conv_out[l], w_attn_out[l], w_o[l], norm2_g[l], w_router[l], b_router[l],
                   w_gate_up[l], b_gate_up[l], w_down[l], b_down[l])
    return x
```

```python
import functools

import jax
import jax.numpy as jnp
from jax import lax
from jax.experimental import pallas as pl
from jax.experimental.pallas import tpu as pltpu

F32 = jnp.float32
BF16 = jnp.bfloat16
I32 = jnp.int32

LANES = 128
SUBLANES = 8

CHUNK = 64
CONV_K = 3
N_HEADS = 8
HEAD_DIM = 64
IDX_HEADS = 4
IDX_DIM = 64
IDX_TOPK = 256
N_EXPERTS = 32
TOP_K = 4
SWIGLU_LIMIT = 7.0
SWIGLU_ALPHA = 1.702
EXPERT_BLOCK = 256
EPS = 1e-6

INT_MIN = -(2 ** 31)
KEY_NEG_INF = -0x7F800000
LOGIT_SHIFT_MAX = 30.0
VMEM_LIMIT = 56 * 1024 * 1024


def _dot(a, b):
    return jnp.dot(a, b, preferred_element_type=F32)


def _dot_nt(a, b):
    return lax.dot_general(a, b, (((1,), (1,)), ((), ())), preferred_element_type=F32)


def _params(sem, vmem=VMEM_LIMIT):
    return pltpu.CompilerParams(dimension_semantics=sem, vmem_limit_bytes=vmem)


def _ada_kernel(c_ref, w_ref, b_ref, o_ref):
    c = c_ref[...]
    s = c * jax.nn.sigmoid(c)
    o_ref[...] = _dot(s.astype(BF16), w_ref[...].astype(BF16)) + b_ref[...]


def _ada(c_pad, w_ada, b_ada):
    rows, d = c_pad.shape
    n = w_ada.shape[1]
    tn = n // 6
    return pl.pallas_call(
        _ada_kernel,
        out_shape=jax.ShapeDtypeStruct((rows, n), F32),
        grid=(n // tn,),
        in_specs=[pl.BlockSpec((rows, d), lambda j: (0, 0)),
                  pl.BlockSpec((d, tn), lambda j: (0, j)),
                  pl.BlockSpec((1, tn), lambda j: (0, j))],
        out_specs=pl.BlockSpec((rows, tn), lambda j: (0, j)),
        compiler_params=_params(("arbitrary",)),
        name="ada",
    )(c_pad, w_ada, b_ada.reshape(1, n))


def _rms_mod(x, g, sc, sh):
    ms = jnp.mean(x * x, axis=-1, keepdims=True)
    y = x * lax.rsqrt(ms + EPS) * g
    return y * (1.0 + sc) + sh


def _head_rms(t, bd, g):
    sq = t * t
    hi = sq.astype(BF16)
    lo = (sq - hi.astype(F32)).astype(BF16)
    ms = _dot(hi, bd) + _dot(lo, bd)
    return t * lax.rsqrt(ms + EPS) * g


def _inproj_kernel(x_ref, sh_ref, sc_ref, g_ref, w_ref, cw_ref, qg_ref, kg_ref, bd_ref, vone_ref,
                   ya_ref, q_ref, k_ref, v_ref, qi_ref, ki_ref, wit_ref, ubuf, *, tm, cw, aw):
    s = pl.program_id(1)
    hb = _rms_mod(x_ref[0], g_ref[...], sc_ref[0], sh_ref[0]).astype(BF16)

    zc = _dot(hb, w_ref[:, 0:3 * cw])
    xin, gb, gc = zc[:, 0:cw], zc[:, cw:2 * cw], zc[:, 2 * cw:3 * cw]
    u = gc * xin

    @pl.when(s == 0)
    def _():
        ubuf[0:SUBLANES, :] = jnp.zeros((SUBLANES, cw), F32)

    ubuf[SUBLANES:SUBLANES + tm, :] = u
    u1 = ubuf[SUBLANES - 1:SUBLANES - 1 + tm, :]
    u2 = ubuf[SUBLANES - 2:SUBLANES - 2 + tm, :]
    conv = cw_ref[0:1, :] * u2 + cw_ref[1:2, :] * u1 + cw_ref[2:3, :] * u
    ya_ref[0] = (gb * conv).astype(BF16)
    ubuf[0:SUBLANES, :] = ubuf[tm:tm + SUBLANES, :]

    o = 3 * cw
    bd = bd_ref[...]
    q = _dot(hb, w_ref[:, o:o + aw])
    q_ref[0] = (_head_rms(q, bd, qg_ref[...]) * (HEAD_DIM ** -0.5)).astype(BF16)
    k = _dot(hb, w_ref[:, o + aw:o + 2 * aw])
    k_ref[0] = _head_rms(k, bd, kg_ref[...]).astype(BF16)
    vw = N_HEADS * LANES
    v_ref[0] = (_dot(hb, w_ref[:, o + 2 * aw:o + 2 * aw + vw]) + vone_ref[...]).astype(BF16)

    o = o + 2 * aw + vw
    qw = IDX_HEADS * LANES
    qi_ref[0] = _dot(hb, w_ref[:, o:o + qw]).astype(BF16)
    ki_ref[0] = _dot(hb, w_ref[:, o + qw:o + qw + LANES]).astype(BF16)
    wi = _dot(hb, w_ref[:, o + qw + LANES:o + qw + 2 * LANES]) * ((IDX_HEADS * IDX_DIM) ** -0.5)
    wit_ref[0] = wi.T[0:SUBLANES, :]


def _inproj(x, sh1, sc1, g1n, w_main, conv_w, qg, kg, bd, vone, *, tm):
    b, s, d = x.shape
    cw = conv_w.shape[1]
    aw = N_HEADS * HEAD_DIM
    vw = N_HEADS * LANES
    qw = IDX_HEADS * LANES
    row = lambda width: pl.BlockSpec((1, tm, width), lambda bi, si: (bi, si, 0))
    mod = pl.BlockSpec((1, 1, d), lambda bi, si: (bi, 0, 0))
    full = lambda a: pl.BlockSpec(a.shape, lambda bi, si: (0,) * a.ndim)
    outs = [jax.ShapeDtypeStruct((b, s, cw), BF16)] + [jax.ShapeDtypeStruct((b, s, aw), BF16)] * 2 + [
        jax.ShapeDtypeStruct((b, s, vw), BF16),
        jax.ShapeDtypeStruct((b, s, qw), BF16), jax.ShapeDtypeStruct((b, s, LANES), BF16),
        jax.ShapeDtypeStruct((b, SUBLANES, s), F32)]
    return pl.pallas_call(
        functools.partial(_inproj_kernel, tm=tm, cw=cw, aw=aw),
        out_shape=outs,
        grid=(b, s // tm),
        in_specs=[row(d), mod, mod, full(g1n), full(w_main), full(conv_w), full(qg), full(kg), full(bd), full(vone)],
        out_specs=[row(cw), row(aw), row(aw), row(vw), row(qw), row(LANES),
                   pl.BlockSpec((1, SUBLANES, tm), lambda bi, si: (bi, 0, si))],
        scratch_shapes=[pltpu.VMEM((tm + SUBLANES, cw), F32)],
        compiler_params=_params(("arbitrary", "arbitrary")),
        name="inproj",
    )(x, sh1, sc1, g1n, w_main, conv_w, qg, kg, bd, vone)


def _sort_key(score):
    bits = pltpu.bitcast(score, I32)
    return jnp.where(bits < 0, INT_MIN - bits, bits)


def _attn_kernel(bound_ref, q_ref, k_ref, v_ref, qi_ref, ki_ref, wit_ref, ut_ref, o_ref,
                 key_scr, keyq_scr, t_scr, need_scr, m_scr, acc_scr, *, qb, topk, online):
    i = pl.program_id(1)
    kt = qb
    ncol = kt // LANES
    nhalf = qb // LANES
    npair = N_HEADS // 2
    neg_inf = -jnp.inf
    slab = 4 * SUBLANES

    qi = qi_ref[0]
    q4 = jnp.concatenate([qi[:, h * LANES:(h + 1) * LANES] for h in range(IDX_HEADS)], axis=0)
    wit = wit_ref[0]
    chunk_end = ((lax.broadcasted_iota(I32, (1, qb), 1) + i * qb) // CHUNK + 1) * CHUNK

    def score_tile(j):
        off = pl.multiple_of(j * kt, kt)
        dots = _dot_nt(ki_ref[0, pl.ds(off, kt), :], q4)
        sc = wit[0:1, :] * jnp.maximum(dots[:, 0:qb], 0.0)
        for h in range(1, IDX_HEADS):
            sc = sc + wit[h:h + 1, :] * jnp.maximum(dots[:, h * qb:(h + 1) * qb], 0.0)
        kpos = lax.broadcasted_iota(I32, (kt, qb), 0) + j * kt
        key = jnp.where(kpos < chunk_end, _sort_key(sc), INT_MIN)
        key_scr[pl.ds(off, kt), :] = key
        keyq_scr[:, pl.ds(off, kt)] = key.T

    def score_pair(jj, carry):
        score_tile(2 * jj)
        score_tile(2 * jj + 1)
        return carry

    lax.fori_loop(0, (i + 2) // 2, score_pair, 0)

    def count_ge(cands):
        wide = [jnp.broadcast_to(c, (slab, LANES)) for c in cands]

        def body(jj, accs):
            off = pl.multiple_of(jj * (2 * kt), 2 * kt)
            accs = list(accs)
            for r in range(2 * kt // slab):
                for g in range(nhalf):
                    kk = key_scr[pl.ds(off + r * slab, slab), g * LANES:(g + 1) * LANES]
                    accs[g] = accs[g] + jnp.where(kk >= wide[g], 1.0, 0.0)
            return tuple(accs)

        accs = lax.fori_loop(0, (i + 2) // 2, body, tuple(jnp.zeros((slab, LANES), F32) for _ in range(nhalf)))
        return [jnp.sum(a, axis=0, keepdims=True) for a in accs]

    def put_rows(ref, vals):
        for g in range(nhalf):
            ref[:, g * LANES:(g + 1) * LANES] = jnp.broadcast_to(vals[g], (LANES, LANES))

    @pl.when(i == 0)
    def _():
        t_scr[...] = jnp.full(t_scr.shape, KEY_NEG_INF, I32)
        need_scr[...] = jnp.full(need_scr.shape, float(topk), F32)

    @pl.when(i > 0)
    def _():
        kf = float(topk)
        zero = jnp.zeros((1, LANES), I32)
        start = tuple(jnp.where(c >= kf, zero, jnp.full((1, LANES), INT_MIN, I32)) for c in count_ge([zero] * nhalf))

        def bit_step(it, res):
            inc = lax.shift_left(jnp.int32(1), jnp.int32(30) - it)
            cands = [r + inc for r in res]
            return tuple(jnp.where(c >= kf, ca, r) for c, ca, r in zip(count_ge(cands), cands, res))

        res = lax.fori_loop(0, 31, bit_step, start)
        put_rows(t_scr, res)
        put_rows(need_scr, [kf - c for c in count_ge([r + 1 for r in res])])

    if online:
        m_scr[...] = jnp.full(m_scr.shape, jnp.finfo(F32).min, F32)
        masked_in = 0.0
    else:
        masked_in = -bound_ref[0]
    acc_scr[...] = jnp.zeros(acc_scr.shape, F32)
    lane = lax.broadcasted_iota(I32, (qb, LANES), 1)
    low_half = lane < HEAD_DIM
    q = q_ref[0]
    zero_q = jnp.zeros((qb, LANES), BF16)
    qm = []
    for h in range(N_HEADS):
        qp = q[:, (h // 2) * LANES:(h // 2 + 1) * LANES]
        qm.append(jnp.where(low_half if h % 2 == 0 else jnp.logical_not(low_half), qp, zero_q))
    qm2 = [jnp.concatenate([qm[2 * p2], qm[2 * p2 + 1]], axis=0) for p2 in range(npair)]
    thr = jnp.concatenate([t_scr[...].T] * ncol, axis=1)
    need = jnp.concatenate([need_scr[...].T] * ncol, axis=1)
    ut = ut_ref[...]

    def attn_tile(j, run):
        off = pl.multiple_of(j * kt, kt)
        key = keyq_scr[:, pl.ds(off, kt)]
        eq = key == thr
        cnt = _dot(jnp.where(eq, 1.0, 0.0).astype(BF16), ut) + run
        sel = jnp.logical_or(key > thr, jnp.logical_and(eq, cnt <= need))
        bias = jnp.where(sel, masked_in, neg_inf)
        for h in range(N_HEADS):
            p2 = h // 2
            if h % 2 == 0:
                s_pair = _dot_nt(qm2[p2], k_ref[0, pl.ds(off, kt), p2 * LANES:(p2 + 1) * LANES])
            vh = v_ref[0, pl.ds(off, kt), h * LANES:(h + 1) * LANES]
            s = s_pair[(h % 2) * qb:(h % 2 + 1) * qb] + bias
            if online:
                m_old = m_scr[h]
                m_new = jnp.maximum(m_old, jnp.broadcast_to(jnp.max(s, axis=1, keepdims=True), (qb, LANES)))
                alpha = jnp.exp(m_old - m_new)
                p = jnp.exp(s - jnp.concatenate([m_new] * ncol, axis=1))
                acc_scr[h] = alpha * acc_scr[h] + _dot(p.astype(BF16), vh)
                m_scr[h] = m_new
            else:
                acc_scr[h] += _dot(jnp.exp(s).astype(BF16), vh)
        return jnp.broadcast_to(cnt[:, kt - 1:kt], (qb, kt))

    lax.fori_loop(0, (i + 2) // 2, lambda jj, run: attn_tile(2 * jj + 1, attn_tile(2 * jj, run)),
                  jnp.zeros((qb, kt), F32))

    def head_out(h):
        a = acc_scr[h]
        return a / jnp.broadcast_to(a[:, HEAD_DIM:HEAD_DIM + 1], (qb, LANES))

    for p2 in range(npair):
        hi = pltpu.roll(head_out(2 * p2 + 1), HEAD_DIM, 1)
        o_ref[0, :, p2 * LANES:(p2 + 1) * LANES] = jnp.where(low_half, head_out(2 * p2), hi).astype(BF16)


def _attn(bound, q, k, v, qi, ki, wit, ut, *, qb, online):
    b, s, aw = q.shape
    topk = min(IDX_TOPK, s // 4)
    assert topk == qb and s % (2 * qb) == 0 and qb % CHUNK == 0
    rowq = lambda width: pl.BlockSpec((1, qb, width), lambda bi, i: (bi, i, 0))
    allk = lambda width: pl.BlockSpec((1, s, width), lambda bi, i: (bi, 0, 0))
    return pl.pallas_call(
        functools.partial(_attn_kernel, qb=qb, topk=topk, online=online),
        out_shape=jax.ShapeDtypeStruct((b, s, aw), BF16),
        grid=(b, s // qb),
        in_specs=[pl.BlockSpec(memory_space=pltpu.SMEM),
                  rowq(aw), allk(aw), allk(v.shape[2]), rowq(qi.shape[2]), allk(LANES),
                  pl.BlockSpec((1, SUBLANES, qb), lambda bi, i: (bi, 0, i)),
                  pl.BlockSpec(ut.shape, lambda bi, i: (0, 0))],
        out_specs=rowq(aw),
        scratch_shapes=[pltpu.VMEM((s + qb, qb), I32), pltpu.VMEM((qb, s), I32), pltpu.VMEM((LANES, qb), I32),
                        pltpu.VMEM((LANES, qb), F32),
                        pltpu.VMEM((N_HEADS, qb, LANES), F32), pltpu.VMEM((N_HEADS, qb, LANES), F32)],
        compiler_params=_params(("arbitrary", "arbitrary")),
        name="attn",
    )(bound, q, k, v, qi, ki, wit, ut)


def _rows_to_tiles(tile_ref, x, rows):
    for sl in range(x.shape[1] // LANES):
        tile_ref[pl.ds(sl, rows, stride=SUBLANES), :] = x[:, sl * LANES:(sl + 1) * LANES]


def _tiles_to_rows(tile_ref, rows, lead=()):
    return jnp.concatenate(
        [tile_ref[lead + (pl.ds(sl, rows, stride=SUBLANES), slice(None))] for sl in range(SUBLANES)], axis=1)


def _tile_rows(t):
    return pl.ds(pl.multiple_of(t * SUBLANES, SUBLANES), SUBLANES)


def _merge_kernel(x_ref, ya_ref, yb_ref, sh1_ref, sc1_ref, g1_ref, sh2_ref, sc2_ref, n1_ref, n2_ref,
                  wg_ref, pa_ref, pb_ref, wo_ref, wr_ref, br_ref, tri_ref,
                  x1_ref, h2_ref, meta_ref, cnt_ref, carry, *, tm, d):
    first = jnp.logical_and(pl.program_id(0) == 0, pl.program_id(1) == 0)

    @pl.when(first)
    def _():
        carry[...] = jnp.zeros(carry.shape, F32)

    x = x_ref[0]
    hb = _rms_mod(x, n1_ref[...], sc1_ref[0], sh1_ref[0]).astype(BF16)
    ga = jax.nn.sigmoid(_dot(hb, wg_ref[:, 0:d]))
    gbm = jax.nn.sigmoid(_dot(hb, wg_ref[:, d:2 * d]))
    merged = ga * _dot(ya_ref[0], pa_ref[...]) + gbm * _dot(yb_ref[0], pb_ref[...])
    x1 = x + g1_ref[0] * _dot(merged.astype(BF16), wo_ref[...])
    x1_ref[0] = x1
    h2 = _rms_mod(x1, n2_ref[...], sc2_ref[0], sh2_ref[0])
    _rows_to_tiles(h2_ref, h2, tm)

    lane = lax.broadcasted_iota(I32, (tm, LANES), 1).astype(F32)
    logits = _dot(h2.astype(BF16), wr_ref[...]) + br_ref[...]
    logits = jnp.where(lane < N_EXPERTS, logits, -jnp.inf)
    vals, idxs = [], []
    hot = jnp.zeros((tm, LANES), F32)
    for _ in range(TOP_K):
        mx = jnp.max(logits, axis=1, keepdims=True)
        ix = jnp.min(jnp.where(logits == mx, lane, float(LANES)), axis=1, keepdims=True)
        pick = lane == ix
        hot = jnp.where(pick, 1.0, hot)
        logits = jnp.where(pick, -jnp.inf, logits)
        vals.append(mx)
        idxs.append(ix)
    es = [jnp.exp(vv - vals[0]) for vv in vals]
    den = es[0] + es[1] + es[2] + es[3]

    prefix = _dot(tri_ref[...], hot.astype(BF16)) + carry[0:1, :]
    carry[...] = jnp.broadcast_to(prefix[tm - 1:tm, :] + hot[tm - 1:tm, :], carry.shape)
    cnt_ref[...] = carry[...]
    meta = jnp.zeros((tm, LANES), F32)
    for kk in range(TOP_K):
        rank = jnp.sum(jnp.where(lane == idxs[kk], prefix, 0.0), axis=1, keepdims=True)
        meta = jnp.where(lane == float(kk), idxs[kk], meta)
        meta = jnp.where(lane == float(TOP_K + kk), rank, meta)
        meta = jnp.where(lane == float(2 * TOP_K + kk), es[kk] / den, meta)
    meta_ref[...] = meta


def _merge(x, ya, yb, mods, n1, n2, w_gate, pa, pb, wo, wr, br, tri, *, tm):
    b, s, d = x.shape
    t = b * s
    sh1, sc1, g1, sh2, sc2 = mods
    nst = s // tm
    row = lambda width: pl.BlockSpec((1, tm, width), lambda bi, si: (bi, si, 0))
    mod = pl.BlockSpec((1, 1, d), lambda bi, si: (bi, 0, 0))
    full = lambda a: pl.BlockSpec(a.shape, lambda bi, si: (0,) * a.ndim)
    outs = [jax.ShapeDtypeStruct((b, s, d), F32),
            jax.ShapeDtypeStruct((t * (d // LANES), LANES), F32),
            jax.ShapeDtypeStruct((t, LANES), F32),
            jax.ShapeDtypeStruct((SUBLANES, LANES), F32)]
    return pl.pallas_call(
        functools.partial(_merge_kernel, tm=tm, d=d),
        out_shape=outs,
        grid=(b, nst),
        in_specs=[row(d), row(ya.shape[2]), row(yb.shape[2]), mod, mod, mod, mod, mod, full(n1), full(n2),
                  full(w_gate), full(pa), full(pb), full(wo), full(wr), full(br), full(tri)],
        out_specs=[row(d),
                   pl.BlockSpec((tm * (d // LANES), LANES), lambda bi, si: (bi * nst + si, 0)),
                   pl.BlockSpec((tm, LANES), lambda bi, si: (bi * nst + si, 0)),
                   pl.BlockSpec((SUBLANES, LANES), lambda bi, si: (0, 0))],
        scratch_shapes=[pltpu.VMEM((SUBLANES, LANES), F32)],
        compiler_params=_params(("arbitrary", "arbitrary")),
        name="merge",
    )(x, ya, yb, sh1, sc1, g1, sh2, sc2, n1, n2, w_gate, pa, pb, wo, wr, br, tri)


def _plan_kernel(meta_ref, cnt_ref, tri_ref, dest_ref, be_ref, pad_ref, *, tm, nbp):
    lane_i = lax.broadcasted_iota(I32, (tm, LANES), 1)
    lane = lane_i.astype(F32)
    nblk = jnp.floor((cnt_ref[...] + float(EXPERT_BLOCK - 1)) * (1.0 / EXPERT_BLOCK))
    pend = _dot(nblk.astype(BF16), tri_ref[...])
    pstart = ((pend - nblk) * float(EXPERT_BLOCK))[0:1, :]
    meta = meta_ref[...]
    dest = jnp.zeros((tm, LANES), F32)
    for kk in range(TOP_K):
        ix = meta[:, kk:kk + 1]
        rank = meta[:, TOP_K + kk:TOP_K + kk + 1]
        base = jnp.sum(jnp.where(lane == ix, pstart, 0.0), axis=1, keepdims=True)
        dest = jnp.where(lane == float(kk), base + rank, dest)
    dest_ref[...] = dest.astype(I32)

    blk = lax.broadcasted_iota(I32, (nbp, LANES), 0).astype(F32)
    lane_b = lax.broadcasted_iota(I32, (nbp, LANES), 1)
    ends = jnp.where(lane_b < N_EXPERTS, pend[0:1, :], float(2 ** 30))
    e_of = jnp.sum(jnp.where(ends <= blk, 1.0, 0.0), axis=1, keepdims=True)
    e_of = jnp.minimum(e_of, float(N_EXPERTS - 1))
    used = jnp.sum(jnp.where(lane_b == N_EXPERTS - 1, pend[0:1, :], 0.0), axis=1, keepdims=True)
    be = jnp.where(lane_b == 0, jnp.broadcast_to(e_of, (nbp, LANES)), jnp.broadcast_to(used, (nbp, LANES)))
    be_ref[...] = be.astype(I32)

    cnt = cnt_ref[0:1, :]
    sub = lax.broadcasted_iota(I32, (SUBLANES, LANES), 0)
    pad_lo = jnp.broadcast_to(pstart + cnt, (SUBLANES, LANES))
    pad_len = jnp.broadcast_to(nblk[0:1, :] * float(EXPERT_BLOCK) - cnt, (SUBLANES, LANES))
    pad_ref[...] = jnp.where(sub == 0, pad_lo, pad_len).astype(I32)


def _plan(meta, counts, tri_e, *, tm, nbp):
    t = meta.shape[0]
    return pl.pallas_call(
        functools.partial(_plan_kernel, tm=tm, nbp=nbp),
        out_shape=[jax.ShapeDtypeStruct((t, LANES), I32), jax.ShapeDtypeStruct((nbp, LANES), I32),
                   jax.ShapeDtypeStruct((SUBLANES, LANES), I32)],
        grid=(t // tm,),
        in_specs=[pl.BlockSpec((tm, LANES), lambda i: (i, 0)),
                  pl.BlockSpec((SUBLANES, LANES), lambda i: (0, 0)),
                  pl.BlockSpec((LANES, LANES), lambda i: (0, 0))],
        out_specs=[pl.BlockSpec((tm, LANES), lambda i: (i, 0)),
                   pl.BlockSpec((nbp, LANES), lambda i: (0, 0)),
                   pl.BlockSpec((SUBLANES, LANES), lambda i: (0, 0))],
        compiler_params=_params(("arbitrary",)),
        name="plan",
    )(meta, counts, tri_e)


def _dispatch_kernel(dest_ref, padlo_ref, padlen_ref, used_ref, h_ref, xs_ref, zbuf, sem, zsem, *, tb, nblocks):
    chunks = [EXPERT_BLOCK >> k for k in range(1, EXPERT_BLOCK.bit_length())]

    @pl.when(pl.program_id(0) == 0)
    def _():
        zbuf[...] = jnp.zeros(zbuf.shape, F32)

        def pad_copy(e, rows):
            n = padlen_ref[e]
            off = padlo_ref[e] + (n // (2 * rows)) * (2 * rows)
            return pltpu.make_async_copy(zbuf.at[pl.ds(0, rows * SUBLANES)],
                                         xs_ref.at[pl.ds(pl.multiple_of(off * SUBLANES, SUBLANES), rows * SUBLANES)],
                                         zsem)

        def each_chunk(act):
            def per_expert(e, c):
                for rows in chunks:
                    @pl.when((padlen_ref[e] & rows) != 0)
                    def _():
                        act(pad_copy(e, rows))
                return c
            lax.fori_loop(0, N_EXPERTS, per_expert, 0)

        def tail_copy(b, half):
            off = pl.multiple_of((b * EXPERT_BLOCK + half * (EXPERT_BLOCK // 2)) * SUBLANES, SUBLANES)
            return pltpu.make_async_copy(zbuf, xs_ref.at[pl.ds(off, zbuf.shape[0])], zsem)

        def each_tail(act):
            def per_block(b, c):
                for half in range(2):
                    act(tail_copy(b, half))
                return c
            lax.fori_loop(used_ref[0], nblocks, per_block, 0)

        each_chunk(lambda cp: cp.start())
        each_tail(lambda cp: cp.start())
        each_chunk(lambda cp: cp.wait())
        each_tail(lambda cp: cp.wait())

    def row_copy(t, kk):
        return pltpu.make_async_copy(h_ref.at[_tile_rows(t)], xs_ref.at[_tile_rows(dest_ref[t * TOP_K + kk])], sem)

    def issue(t, c):
        for kk in range(TOP_K):
            row_copy(t, kk).start(priority=kk % 2)
        return c

    lax.fori_loop(0, tb, issue, 0, unroll=4)

    def drain(t, c):
        for kk in range(TOP_K):
            row_copy(t, kk).wait()
        return c

    lax.fori_loop(0, tb, drain, 0, unroll=8)


def _dispatch(dest_flat, pad_lo, pad_len, used, h2, cap, *, tb):
    t = h2.shape[0] // SUBLANES
    return pl.pallas_call(
        functools.partial(_dispatch_kernel, tb=tb, nblocks=cap // EXPERT_BLOCK),
        out_shape=jax.ShapeDtypeStruct((cap * SUBLANES, LANES), F32),
        grid=(t // tb,),
        in_specs=[pl.BlockSpec((tb * TOP_K,), lambda i: (i,), memory_space=pltpu.SMEM),
                  pl.BlockSpec(memory_space=pltpu.SMEM),
                  pl.BlockSpec(memory_space=pltpu.SMEM),
                  pl.BlockSpec(memory_space=pltpu.SMEM),
                  pl.BlockSpec((tb * SUBLANES, LANES), lambda i: (i, 0))],
        out_specs=pl.BlockSpec(memory_space=pl.ANY),
        scratch_shapes=[pltpu.VMEM((EXPERT_BLOCK // 2 * SUBLANES, LANES), F32),
                        pltpu.SemaphoreType.DMA(()), pltpu.SemaphoreType.DMA(())],
        compiler_params=_params(("arbitrary",)),
        name="dispatch",
    )(dest_flat, pad_lo, pad_len, used, h2)


def _ffn_kernel(be_ref, used_ref, nxt_ref, xs_ref, wgu_hbm, bgu_ref, wd_hbm, bd_ref, ys_ref,
                wgu_f32, wd_f32, wgu_bf, wd_bf, state, sems, *, d, dff):
    i = pl.program_id(0)
    e = be_ref[i]

    def fetch(expert, slot):
        return (pltpu.make_async_copy(wgu_hbm.at[expert], wgu_f32.at[slot], sems.at[0, slot]),
                pltpu.make_async_copy(wd_hbm.at[expert], wd_f32.at[slot], sems.at[1, slot]))

    @pl.when(i == 0)
    def _():
        state[0] = -1
        state[1] = 1
        for cp in fetch(e, 0):
            cp.start()

    @pl.when(i < used_ref[0])
    def _():
        @pl.when(state[0] != e)
        def _():
            slot = 1 - state[1]
            for cp in fetch(e, slot):
                cp.wait()
            wgu_bf[...] = wgu_f32[slot].astype(BF16)
            wd_bf[...] = wd_f32[slot].astype(BF16)
            state[0] = e
            state[1] = slot

            @pl.when(nxt_ref[i] >= 0)
            def _():
                for cp in fetch(nxt_ref[i], 1 - slot):
                    cp.start()

        xb = _tiles_to_rows(xs_ref, EXPERT_BLOCK).astype(BF16)
        gu = _dot(xb, wgu_bf[...]) + bgu_ref[0]
        x_glu = jnp.minimum(gu[:, 0:dff], SWIGLU_LIMIT)
        x_lin = jnp.clip(gu[:, dff:2 * dff], -SWIGLU_LIMIT, SWIGLU_LIMIT)
        act = x_glu * jax.nn.sigmoid(SWIGLU_ALPHA * x_glu) * (x_lin + 1.0)
        y = _dot(act.astype(BF16), wd_bf[...]) + bd_ref[0]
        _rows_to_tiles(ys_ref, y, EXPERT_BLOCK)

    @pl.when(i >= used_ref[0])
    def _():
        ys_ref[...] = jnp.zeros(ys_ref.shape, F32)


def _ffn(block_e, used, next_e, xs, w_gate_up, b_gate_up, w_down, b_down):
    ne, d, dff2 = w_gate_up.shape
    dff = dff2 // 2
    blk = EXPERT_BLOCK * SUBLANES
    nblocks = xs.shape[0] // blk
    grid_spec = pltpu.PrefetchScalarGridSpec(
        num_scalar_prefetch=3,
        grid=(nblocks,),
        in_specs=[pl.BlockSpec((blk, LANES), lambda i, be, us, nx: (jnp.minimum(i, us[0] - 1), 0)),
                  pl.BlockSpec(memory_space=pl.ANY),
                  pl.BlockSpec((1, 1, dff2), lambda i, be, us, nx: (be[i], 0, 0)),
                  pl.BlockSpec(memory_space=pl.ANY),
                  pl.BlockSpec((1, 1, d), lambda i, be, us, nx: (be[i], 0, 0))],
        out_specs=pl.BlockSpec((blk, LANES), lambda i, be, us, nx: (i, 0)),
        scratch_shapes=[pltpu.VMEM((2, d, dff2), F32), pltpu.VMEM((2, dff, d), F32),
                        pltpu.VMEM((d, dff2), BF16), pltpu.VMEM((dff, d), BF16),
                        pltpu.SMEM((2,), I32), pltpu.SemaphoreType.DMA((2, 2))])
    return pl.pallas_call(
        functools.partial(_ffn_kernel, d=d, dff=dff),
        out_shape=jax.ShapeDtypeStruct(xs.shape, F32),
        grid_spec=grid_spec,
        compiler_params=_params(("arbitrary",)),
        name="ffn",
    )(block_e, used, next_e, xs, w_gate_up, b_gate_up.reshape(ne, 1, dff2), w_down, b_down.reshape(ne, 1, d))


def _combine_kernel(dest_ref, dest_next_ref, ys_ref, x1_ref, g2_ref, meta_ref, o_ref, buf, sems, *, tb, nsteps):
    step = pl.program_id(0) * pl.num_programs(1) + pl.program_id(1)
    slot = step % 2

    def row_copy(idx_ref, sl, t, kk):
        return pltpu.make_async_copy(ys_ref.at[_tile_rows(idx_ref[t * TOP_K + kk])],
                                     buf.at[sl, kk, _tile_rows(t)], sems.at[sl])

    def gather(idx_ref, sl):
        def issue(t, c):
            for kk in range(TOP_K):
                row_copy(idx_ref, sl, t, kk).start(priority=kk % 2)
            return c
        lax.fori_loop(0, tb, issue, 0, unroll=8)

    @pl.when(step == 0)
    def _():
        gather(dest_ref, 0)

    @pl.when(step + 1 < nsteps)
    def _():
        gather(dest_next_ref, 1 - slot)

    def drain(t, c):
        for kk in range(TOP_K):
            row_copy(dest_ref, slot, t, kk).wait()
        return c

    lax.fori_loop(0, tb, drain, 0, unroll=8)

    meta = meta_ref[...]
    moe = None
    for kk in range(TOP_K):
        yk = _tiles_to_rows(buf, tb, lead=(slot, kk))
        term = yk * meta[:, 2 * TOP_K + kk:2 * TOP_K + kk + 1]
        moe = term if moe is None else moe + term
    o_ref[0] = x1_ref[0] + g2_ref[0] * moe


def _combine(dest_flat, ys, x1, g2, meta, *, tb):
    b, s, d = x1.shape
    nst = s // tb
    nsteps = b * nst
    return pl.pallas_call(
        functools.partial(_combine_kernel, tb=tb, nsteps=nsteps),
        out_shape=jax.ShapeDtypeStruct((b, s, d), F32),
        grid=(b, nst),
        in_specs=[pl.BlockSpec((tb * TOP_K,), lambda bi, si: (bi * nst + si,), memory_space=pltpu.SMEM),
                  pl.BlockSpec((tb * TOP_K,), lambda bi, si: (jnp.minimum(bi * nst + si + 1, nsteps - 1),),
                               memory_space=pltpu.SMEM),
                  pl.BlockSpec(memory_space=pl.ANY),
                  pl.BlockSpec((1, tb, d), lambda bi, si: (bi, si, 0)),
                  pl.BlockSpec((1, 1, d), lambda bi, si: (bi, 0, 0)),
                  pl.BlockSpec((tb, LANES), lambda bi, si: (bi * nst + si, 0))],
        out_specs=pl.BlockSpec((1, tb, d), lambda bi, si: (bi, si, 0)),
        scratch_shapes=[pltpu.VMEM((2, TOP_K, tb * SUBLANES, LANES), F32), pltpu.SemaphoreType.DMA((2,))],
        compiler_params=_params(("arbitrary", "arbitrary")),
        name="combine",
    )(dest_flat, dest_flat, ys, x1, g2, meta)


def _tri(n, strict, upper):
    r = lax.broadcasted_iota(I32, (n, n), 0)
    c = lax.broadcasted_iota(I32, (n, n), 1)
    if upper:
        m = (r < c) if strict else (r <= c)
    else:
        m = (c < r) if strict else (c <= r)
    return m.astype(BF16)


def _layer(x, c_pad, w_ada, b_ada, norm1_g, w_in, conv_w, q_norm_g, k_norm_g, w_conv_out, w_attn_out, w_o,
           norm2_g, w_router, b_router, w_gate_up, b_gate_up, w_down, b_down):
    b, s, d = x.shape
    assert d == SUBLANES * LANES
    t = b * s
    cw = conv_w.shape[1]
    aw = N_HEADS * HEAD_DIM
    tm = min(512, s)
    qb = min(IDX_TOPK, s // 4)
    tb = 256

    mod = _ada(c_pad, w_ada, b_ada)[:b]
    sh1, sc1, g1, sh2, sc2, g2 = [m.reshape(b, 1, d) for m in jnp.split(mod, 6, axis=-1)]

    o = 3 * cw + 3 * aw
    w_qi = w_in[:, o:o + IDX_HEADS * IDX_DIM].reshape(d, IDX_HEADS, IDX_DIM)
    w_qi = jnp.pad(w_qi, ((0, 0), (0, 0), (0, LANES - IDX_DIM))).reshape(d, IDX_HEADS * LANES)
    o2 = o + IDX_HEADS * IDX_DIM
    w_ki = jnp.pad(w_in[:, o2:o2 + IDX_DIM], ((0, 0), (0, LANES - IDX_DIM)))
    o3 = o2 + IDX_DIM
    w_wi = jnp.pad(w_in[:, o3:o3 + IDX_HEADS], ((0, 0), (0, LANES - IDX_HEADS)))
    o4 = o3 + IDX_HEADS
    ov = 3 * cw + 2 * aw
    w_v = jnp.pad(w_in[:, ov:ov + aw].reshape(d, N_HEADS, HEAD_DIM), ((0, 0), (0, 0), (0, LANES - HEAD_DIM)))
    vone = (lax.broadcasted_iota(I32, (1, N_HEADS * LANES), 1) % LANES == HEAD_DIM).astype(F32)
    w_main = jnp.concatenate([w_in[:, :ov], w_v.reshape(d, N_HEADS * LANES), w_qi, w_ki, w_wi], axis=1).astype(BF16)
    w_gate = w_in[:, o4:o4 + 2 * d].astype(BF16)
    seg = lax.broadcasted_iota(I32, (aw, aw), 0) // HEAD_DIM == lax.broadcasted_iota(I32, (aw, aw), 1) // HEAD_DIM
    bd = jnp.where(seg, 1.0 / HEAD_DIM, 0.0).astype(BF16)
    qg = jnp.tile(q_norm_g, N_HEADS).reshape(1, aw)
    kg = jnp.tile(k_norm_g, N_HEADS).reshape(1, aw)

    ya, q, k, v, qi, ki, wi = _inproj(x, sh1, sc1, norm1_g.reshape(1, d), w_main, conv_w, qg, kg, bd, vone, tm=tm)
    bound = (HEAD_DIM ** 0.5 * 1.01 * jnp.max(jnp.abs(q_norm_g)) * jnp.max(jnp.abs(k_norm_g))).reshape(1)
    ut = _tri(qb, strict=False, upper=True)
    yb = lax.cond(bound[0] <= LOGIT_SHIFT_MAX,
                  lambda: _attn(bound, q, k, v, qi, ki, wi, ut, qb=qb, online=False),
                  lambda: _attn(bound, q, k, v, qi, ki, wi, ut, qb=qb, online=True))

    wr = jnp.pad(w_router, ((0, 0), (0, LANES - N_EXPERTS))).astype(BF16)
    br = jnp.pad(b_router, (0, LANES - N_EXPERTS)).reshape(1, LANES)
    x1, h2, meta, counts = _merge(
        x, ya, yb, (sh1, sc1, g1, sh2, sc2), norm1_g.reshape(1, d), norm2_g.reshape(1, d), w_gate,
        w_conv_out.astype(BF16), w_attn_out.astype(BF16), w_o.astype(BF16), wr, br,
        _tri(tm, strict=True, upper=False), tm=tm)

    nk = t * TOP_K
    cap = -(-nk // EXPERT_BLOCK) * EXPERT_BLOCK + N_EXPERTS * EXPERT_BLOCK
    nblocks = cap // EXPERT_BLOCK
    nbp = -(-nblocks // SUBLANES) * SUBLANES
    dest_pad, be_pad, pad = _plan(meta, counts, _tri(LANES, strict=False, upper=True), tm=min(4 * tm, t), nbp=nbp)
    dest_flat = dest_pad[:, :TOP_K].reshape(-1)
    block_e = be_pad[:nblocks, 0]
    used = be_pad[0:1, 1]

    xs = _dispatch(dest_flat, pad[0, :N_EXPERTS], pad[1, :N_EXPERTS], used, h2, cap, tb=tb)
    later = jnp.logical_and(block_e[None, :] > block_e[:, None], jnp.arange(nblocks)[None, :] < used[0])
    next_e = jnp.min(jnp.where(later, block_e[None, :], N_EXPERTS), axis=1)
    next_e = jnp.where(next_e < N_EXPERTS, next_e, -1).astype(I32)
    ys = _ffn(block_e, used, next_e, xs, w_gate_up, b_gate_up, w_down, b_down)
    return _combine(dest_flat, ys, x1, g2, meta, tb=tb)


def kernel(x, c, w_ada, b_ada, norm1_g, w_in, conv_w, q_norm_g, k_norm_g, w_conv_out, w_attn_out, w_o, norm2_g,
           w_router, b_router, w_gate_up, b_gate_up, w_down, b_down):
    c_pad = jnp.pad(c, ((0, -c.shape[0] % SUBLANES), (0, 0)))
    for l in range(w_ada.shape[0]):
        x = _layer(x, c_pad, w_ada[l], b_ada[l], norm1_g[l], w_in[l], conv_w[l], q_norm_g[l], k_norm_g[l],
                   w_conv_out[l], w_attn_out[l], w_o[l], norm2_g[l], w_router[l], b_router[l],
                   w_gate_up[l], b_gate_up[l], w_down[l], b_down[l])
    return x
```
